```python
import math
import jax, jax.numpy as jnp
from jax import lax
import numpy as np

D_MODEL = 2048
BATCH = 16
SEQ = 2048
DEPTH = 2
DEC_BATCH = 16
DEC_SEQ = 32
PAST_LEN = 2048

CHUNK = 64
Q_BLOCK = 128
N_MIXERS = 2
N_RG = (DEPTH + 1) // 2
N_ATTN = DEPTH // 2
D_RNN = D_MODEL
RG_BLOCKS = 16
RG_BLK = D_RNN // RG_BLOCKS
RG_CONV_W = 4
RG_C = 8.0
N_HEADS = 8
HEAD_DIM = D_MODEL // (2 * N_HEADS)
D_ATT = 2 * N_HEADS * HEAD_DIM
D_FF = 5632
FFN_CONV_W = 3
EPS = 1e-6
NEG_INF = -1e30

kernel_name = "hybrid_rglru_diffattn_convffn_stream_step"


def rms_norm(x, g):
    xf = x.astype(jnp.float32)
    y = xf * lax.rsqrt(jnp.mean(xf * xf, axis=-1, keepdims=True) + EPS)
    return (y * g.astype(jnp.float32)).astype(x.dtype)


def causal_dwconv(x_full, w, b, t):
    width = w.shape[0]
    out = b + x_full[:, 0:t] * w[0]
    for k in range(1, width):
        out = out + x_full[:, k:k + t] * w[k]
    return out


def linear_recurrence(a, b, h0):
    b = b.at[:, 0].add(a[:, 0] * h0)

    def combine(e1, e2):
        a1, b1 = e1
        a2, b2 = e2
        return a1 * a2, a2 * b1 + b2

    _, h = lax.associative_scan(combine, (a, b), axis=1)
    return h


def rglru_mixer(x, conv_hist, h0, w_in, conv_w, conv_b, gate_w, gate_b, log_lam, w_out):
    bsz, t, _ = x.shape
    u = x @ w_in
    gate_br, rec = u[..., :D_RNN], u[..., D_RNN:]
    y = jax.nn.gelu(gate_br)
    full = jnp.concatenate([conv_hist.astype(rec.dtype), rec], axis=1)
    xc = causal_dwconv(full, conv_w, conv_b, t)
    new_hist = full[:, -(RG_CONV_W - 1):]
    g = jnp.einsum('btnc,ncd->btnd', xc.reshape(bsz, t, RG_BLOCKS, RG_BLK), gate_w) + gate_b
    g = jax.nn.sigmoid(g.astype(jnp.float32))
    r = g[..., :RG_BLK].reshape(bsz, t, D_RNN)
    i = g[..., RG_BLK:].reshape(bsz, t, D_RNN)
    log_a = -RG_C * r * jax.nn.softplus(-log_lam.astype(jnp.float32))
    a = jnp.exp(log_a)
    mult = jnp.sqrt(-jnp.expm1(2.0 * log_a))
    b = mult * i * xc.astype(jnp.float32)
    h = linear_recurrence(a, b, h0.astype(jnp.float32))
    out = (h.astype(x.dtype) * y) @ w_out
    return out, new_hist, h[:, -1]


def diff_lambda(lam, lambda_init):
    lam = lam.astype(jnp.float32)
    return jnp.exp(jnp.sum(lam[0] * lam[1])) - jnp.exp(jnp.sum(lam[2] * lam[3])) + lambda_init


def diff_qkv(x, w_qkv):
    bsz, t, _ = x.shape
    qkv = x @ w_qkv
    q = qkv[..., :D_ATT].reshape(bsz, t, N_HEADS, 2, HEAD_DIM)
    k = qkv[..., D_ATT:2 * D_ATT].reshape(bsz, t, N_HEADS, 2, HEAD_DIM)
    v = qkv[..., 2 * D_ATT:].reshape(bsz, t, N_HEADS, 2 * HEAD_DIM)
    return q, k, v


def diff_attend(q, k, v, mask, lam):
    s = jnp.einsum('bqhjd,bkhjd->bhjqk', q, k).astype(jnp.float32) * (HEAD_DIM ** -0.5)
    if mask is not None:
        s = jnp.where(mask, s, NEG_INF)
    p = jax.nn.softmax(s, axis=-1)
    attn = p[:, :, 0] - lam * p[:, :, 1]
    return jnp.einsum('bhqk,bkhe->bqhe', attn.astype(v.dtype), v)


def diff_out(o, subln, lambda_init, w_out):
    bsz, t = o.shape[:2]
    o = rms_norm(o, subln) * (1.0 - lambda_init)
    return o.reshape(bsz, t, D_ATT) @ w_out


def diff_attn_prompt(x, w_qkv, lam_p, subln, w_out, lambda_init):
    bsz, t, _ = x.shape
    q, k, v = diff_qkv(x, w_qkv)
    lam = diff_lambda(lam_p, lambda_init)
    n_blk = t // Q_BLOCK
    q_blocks = jnp.moveaxis(q.reshape(bsz, n_blk, Q_BLOCK, N_HEADS, 2, HEAD_DIM), 1, 0)
    key_chunk = jnp.arange(t) // CHUNK

    def one_block(args):
        qb, blk = args
        q_chunk = (blk * Q_BLOCK + jnp.arange(Q_BLOCK)) // CHUNK
        mask = key_chunk[None, :] <= q_chunk[:, None]
        return diff_attend(qb, k, v, mask, lam)

    o = lax.map(one_block, (q_blocks, jnp.arange(n_blk)))
    o = jnp.moveaxis(o, 0, 1).reshape(bsz, t, N_HEADS, 2 * HEAD_DIM)
    return diff_out(o, subln, lambda_init, w_out), k.reshape(bsz, t, 2 * N_HEADS, HEAD_DIM), v


def diff_attn_sample(x, cache_k, cache_v, w_qkv, lam_p, subln, w_out, lambda_init):
    bsz, t, _ = x.shape
    past = cache_k.shape[1]
    q, k, v = diff_qkv(x, w_qkv)
    lam = diff_lambda(lam_p, lambda_init)
    k_all = jnp.concatenate(
        [cache_k.astype(k.dtype).reshape(bsz, past, N_HEADS, 2, HEAD_DIM), k], axis=1)
    v_all = jnp.concatenate([cache_v.astype(v.dtype), v], axis=1)
    o = diff_attend(q, k_all, v_all, None, lam)
    return diff_out(o, subln, lambda_init, w_out), k.reshape(bsz, t, 2 * N_HEADS, HEAD_DIM), v


def conv_ffn(x, hist, w_up, conv_w, conv_b, w_down):
    t = x.shape[1]
    u = x @ w_up
    full = jnp.concatenate([hist.astype(u.dtype), u], axis=1)
    c = causal_dwconv(full, conv_w, conv_b, t)
    h = jax.nn.gelu(c[..., :D_FF]) * c[..., D_FF:]
    return h @ w_down, full[:, -(FFN_CONV_W - 1):]


def setup_inputs(seed: int = 0) -> dict:
    key = jax.random.key(seed)
    ks = jax.random.split(key, 32)
    f32 = jnp.float32
    nrm = lambda k, shape, s: jax.random.normal(k, shape, f32) * s
    u_lam = jax.random.uniform(ks[13], (N_RG, D_RNN), f32, minval=0.9, maxval=0.999)
    return {
        "x_prompt": nrm(ks[0], (BATCH, SEQ, D_MODEL), 1.0),
        "x_sample": nrm(ks[1], (DEC_BATCH, DEC_SEQ, D_MODEL), 1.0),
        "state_rglru_conv": nrm(ks[2], (N_RG, DEC_BATCH, RG_CONV_W - 1, D_RNN), 1.0),
        "state_rglru_h": nrm(ks[3], (N_RG, DEC_BATCH, D_RNN), 0.5),
        "cache_attn_k": nrm(ks[4], (N_ATTN, DEC_BATCH, PAST_LEN, 2 * N_HEADS, HEAD_DIM), 1.0),
        "cache_attn_v": nrm(ks[5], (N_ATTN, DEC_BATCH, PAST_LEN, N_HEADS, 2 * HEAD_DIM), 1.0),
        "state_ffn_conv": nrm(ks[6], (DEPTH, DEC_BATCH, FFN_CONV_W - 1, 2 * D_FF), 1.0),
        "rg_norm": 1.0 + nrm(ks[7], (N_RG, D_MODEL), 0.02),
        "rg_w_in": nrm(ks[8], (N_RG, D_MODEL, 2 * D_RNN), D_MODEL ** -0.5),
        "rg_conv_w": nrm(ks[9], (N_RG, RG_CONV_W, D_RNN), RG_CONV_W ** -0.5),
        "rg_conv_b": nrm(ks[10], (N_RG, D_RNN), 0.01),
        "rg_gate_w": nrm(ks[11], (N_RG, RG_BLOCKS, RG_BLK, 2 * RG_BLK), RG_BLK ** -0.5),
        "rg_gate_b": nrm(ks[12], (N_RG, RG_BLOCKS, 2 * RG_BLK), 0.01),
        "rg_log_lambda": jnp.log(u_lam) - jnp.log1p(-u_lam),
        "rg_w_out": nrm(ks[14], (N_RG, D_RNN, D_MODEL), D_RNN ** -0.5),
        "at_norm": 1.0 + nrm(ks[15], (N_ATTN, D_MODEL), 0.02),
        "at_w_qkv": nrm(ks[16], (N_ATTN, D_MODEL, 3 * D_ATT), D_MODEL ** -0.5),
        "at_lambda": nrm(ks[17], (N_ATTN, 4, HEAD_DIM), 0.1),
        "at_subln": 1.0 + nrm(ks[18], (N_ATTN, 2 * HEAD_DIM), 0.02),
        "at_w_out": nrm(ks[19], (N_ATTN, D_ATT, D_MODEL), D_ATT ** -0.5),
        "ffn_norm": 1.0 + nrm(ks[20], (DEPTH, D_MODEL), 0.02),
        "ffn_w_up": nrm(ks[21], (DEPTH, D_MODEL, 2 * D_FF), D_MODEL ** -0.5),
        "ffn_conv_w": nrm(ks[22], (DEPTH, FFN_CONV_W, 2 * D_FF), FFN_CONV_W ** -0.5),
        "ffn_conv_b": nrm(ks[23], (DEPTH, 2 * D_FF), 0.01),
        "ffn_w_down": nrm(ks[24], (DEPTH, D_FF, D_MODEL), D_FF ** -0.5),
        "final_norm": 1.0 + nrm(ks[25], (D_MODEL,), 0.02),
    }


def reference(x_prompt, x_sample, state_rglru_conv, state_rglru_h, cache_attn_k, cache_attn_v,
              state_ffn_conv, rg_norm, rg_w_in, rg_conv_w, rg_conv_b, rg_gate_w, rg_gate_b,
              rg_log_lambda, rg_w_out, at_norm, at_w_qkv, at_lambda, at_subln, at_w_out,
              ffn_norm, ffn_w_up, ffn_conv_w, ffn_conv_b, ffn_w_down, final_norm):
    xp, xs = x_prompt, x_sample
    bp = xp.shape[0]
    p_rg_conv, p_rg_h, p_k, p_v, p_ffn = [], [], [], [], []
    s_rg_conv, s_rg_h, s_k, s_v, s_ffn = [], [], [], [], []
    for layer in range(DEPTH):
        j = layer // N_MIXERS
        if layer % N_MIXERS == 0:
            w = (rg_w_in[j], rg_conv_w[j], rg_conv_b[j], rg_gate_w[j], rg_gate_b[j],
                 rg_log_lambda[j], rg_w_out[j])
            zero_hist = jnp.zeros((bp, RG_CONV_W - 1, D_RNN), xp.dtype)
            zero_h = jnp.zeros((bp, D_RNN), jnp.float32)
            yp, cp, hp = rglru_mixer(rms_norm(xp, rg_norm[j]), zero_hist, zero_h, *w)
            ys, cs, hs = rglru_mixer(rms_norm(xs, rg_norm[j]), state_rglru_conv[j],
                                     state_rglru_h[j], *w)
            p_rg_conv.append(cp); p_rg_h.append(hp)
            s_rg_conv.append(cs); s_rg_h.append(hs)
        else:
            lambda_init = 0.8 - 0.6 * math.exp(-0.3 * layer)
            yp, kp, vp = diff_attn_prompt(rms_norm(xp, at_norm[j]), at_w_qkv[j], at_lambda[j],
                                          at_subln[j], at_w_out[j], lambda_init)
            ys, kn, vn = diff_attn_sample(rms_norm(xs, at_norm[j]), cache_attn_k[j], cache_attn_v[j],
                                          at_w_qkv[j], at_lambda[j], at_subln[j], at_w_out[j],
                                          lambda_init)
            p_k.append(kp); p_v.append(vp)
            s_k.append(kn); s_v.append(vn)
        xp = xp + yp
        xs = xs + ys
        zero_ffn = jnp.zeros((bp, FFN_CONV_W - 1, 2 * D_FF), xp.dtype)
        fp, fcp = conv_ffn(rms_norm(xp, ffn_norm[layer]), zero_ffn, ffn_w_up[layer],
                           ffn_conv_w[layer], ffn_conv_b[layer], ffn_w_down[layer])
        fs, fcs = conv_ffn(rms_norm(xs, ffn_norm[layer]), state_ffn_conv[layer], ffn_w_up[layer],
                           ffn_conv_w[layer], ffn_conv_b[layer], ffn_w_down[layer])
        p_ffn.append(fcp); s_ffn.append(fcs)
        xp = xp + fp
        xs = xs + fs
    y_prompt = rms_norm(xp, final_norm)
    y_sample = rms_norm(xs, final_norm)
    return (y_prompt, y_sample,
            jnp.stack(p_rg_conv), jnp.stack(p_rg_h), jnp.stack(p_k), jnp.stack(p_v), jnp.stack(p_ffn),
            jnp.stack(s_rg_conv), jnp.stack(s_rg_h), jnp.stack(s_k), jnp.stack(s_v), jnp.stack(s_ffn))
```

```python
import functools
import math

import jax
import jax.numpy as jnp
from jax import lax
from jax.experimental import pallas as pl
from jax.experimental.pallas import tpu as pltpu

F32 = jnp.float32
BF16 = jnp.bfloat16

EPS = 1e-6
NEG_INF = -1e30
CHUNK = 64
N_HEADS = 8
RG_BLOCKS = 16
RG_C = 8.0
SUBLANES = 8
VMEM_LIMIT_BYTES = 56 * 1024 * 1024

ROW_TILE = 512
RG_COL_TILE = 512
FF_COL_TILE = 512
PROJ_COL_TILE = 512
ATTN_TILE = 512


def _params(n_axes):
    return pltpu.CompilerParams(
        dimension_semantics=("arbitrary",) * n_axes,
        vmem_limit_bytes=VMEM_LIMIT_BYTES,
    )


def _rmsnorm_rows(x, g):
    ms = jnp.mean(x * x, axis=-1, keepdims=True)
    return (x * lax.rsqrt(ms + EPS)) * g


def _row_tiling(n_seq, t):
    if t >= ROW_TILE:
        assert t % ROW_TILE == 0
        return ROW_TILE, 1, t // ROW_TILE
    assert ROW_TILE % t == 0 and t % SUBLANES == 0
    nseg = min(ROW_TILE // t, n_seq)
    assert n_seq % nseg == 0
    return t, nseg, 1


def _last_tile(state, n_seq, tiles_per_seq):
    if tiles_per_seq == 1:
        return state
    return state.reshape((n_seq, tiles_per_seq) + state.shape[1:])[:, -1]


def _rg_kernel(x_ref, hist_ref, h0_ref, g_ref, wg_ref, wr_ref, cw_ref, cb_ref, gw_ref, gb_ref,
               ll_ref, wo_ref,
               out_ref, nh_ref, hl_ref,
               xn_scr, acc_scr, rbuf, xc_scr, a_scr, b_scr, h_scr, ccar, hcar,
               *, tseg, nseg, tiles_per_seq, blk):
    r = pl.program_id(0)
    c = pl.program_id(1)
    n_c = pl.num_programs(1)
    conv_w = cw_ref.shape[0]
    tc = wg_ref.shape[1]
    tm = tseg * nseg
    tb = r % tiles_per_seq
    pad = SUBLANES - (conv_w - 1)

    @pl.when(c == 0)
    def _():
        xn_scr[...] = _rmsnorm_rows(x_ref[...], g_ref[...]).astype(BF16)
        acc_scr[...] = jnp.zeros_like(acc_scr)

    xn = xn_scr[...]
    y = jax.nn.gelu(jnp.dot(xn, wg_ref[...], preferred_element_type=F32))
    rec = jnp.dot(xn, wr_ref[...], preferred_element_type=F32)

    for s in range(nseg):
        rbuf[s, SUBLANES:SUBLANES + tseg, :] = rec[s * tseg:(s + 1) * tseg]
        if tiles_per_seq == 1:
            rbuf[s, pad:SUBLANES, :] = hist_ref[s]
        else:
            @pl.when(tb == 0)
            def _():
                rbuf[s, pad:SUBLANES, :] = hist_ref[s]

            @pl.when(tb != 0)
            def _():
                rbuf[s, pad:SUBLANES, :] = ccar[c]
        xc = cb_ref[...] + rbuf[s, pad:pad + tseg, :] * cw_ref[0:1, :]
        for k in range(1, conv_w):
            xc = xc + rbuf[s, pad + k:pad + k + tseg, :] * cw_ref[k:k + 1, :]
        xc_scr[s * tseg:(s + 1) * tseg, :] = xc
        new_hist = rbuf[s, pad + tseg:SUBLANES + tseg, :]
        nh_ref[s] = new_hist
        if tiles_per_seq > 1:
            ccar[c] = new_hist

    z = -ll_ref[...]
    softplus = jnp.maximum(z, 0.0) + jnp.log1p(jnp.exp(-jnp.abs(z)))
    for n in range(tc // blk):
        cols = slice(n * blk, (n + 1) * blk)
        xc_n = xc_scr[:, cols]
        gate = jnp.dot(xc_n.astype(BF16), gw_ref[n], preferred_element_type=F32) + gb_ref[n]
        gate = jax.nn.sigmoid(gate)
        log_a = (-RG_C * gate[:, :blk]) * softplus[:, cols]
        a_n = jnp.exp(log_a)
        a_scr[:, cols] = a_n
        b_scr[:, cols] = jnp.sqrt(1.0 - a_n * a_n) * gate[:, blk:] * xc_n

    a = a_scr[...].reshape(tm // SUBLANES, SUBLANES, tc)
    b = b_scr[...].reshape(tm // SUBLANES, SUBLANES, tc)
    row = lax.broadcasted_iota(jnp.int32, a.shape, 1)
    shift = 1
    while shift < SUBLANES:
        keep = row >= shift
        a_prev = pltpu.roll(a, shift, axis=1)
        b_prev = pltpu.roll(b, shift, axis=1)
        b = jnp.where(keep, a * b_prev + b, b)
        a = jnp.where(keep, a * a_prev, a)
        shift *= 2
    a_scr[...] = a.reshape(tm, tc)
    b_scr[...] = b.reshape(tm, tc)

    for s in range(nseg):
        if tiles_per_seq == 1:
            h_first = h0_ref[s]
        else:
            @pl.when(tb == 0)
            def _():
                hcar[c] = h0_ref[s]
            h_first = hcar[c]

        def group(i, h_prev, s=s):
            off = pl.multiple_of(s * tseg + i * SUBLANES, SUBLANES)
            h = a_scr[pl.ds(off, SUBLANES), :] * h_prev + b_scr[pl.ds(off, SUBLANES), :]
            h_scr[pl.ds(off, SUBLANES), :] = h
            return h[SUBLANES - 1:SUBLANES, :]

        h_last = lax.fori_loop(0, tseg // SUBLANES, group, h_first, unroll=8)
        hl_ref[s] = h_last
        if tiles_per_seq > 1:
            hcar[c] = h_last

    acc_scr[...] += jnp.dot((h_scr[...] * y).astype(BF16), wo_ref[...], preferred_element_type=F32)

    @pl.when(c == n_c - 1)
    def _():
        out_ref[...] = x_ref[...] + acc_scr[...]


def _rg_layer(x, hist, h0, t, norm_g, w_in, conv_w, conv_b, gate_w, gate_b, log_lam, w_out):
    m, d = x.shape
    n_seq = m // t
    c_rnn = w_out.shape[0]
    conv_width = conv_w.shape[0]
    blk = c_rnn // RG_BLOCKS
    tc = min(RG_COL_TILE, c_rnn)
    n_c = c_rnn // tc
    tseg, nseg, tps = _row_tiling(n_seq, t)
    tm = tseg * nseg
    kern = functools.partial(_rg_kernel, tseg=tseg, nseg=nseg, tiles_per_seq=tps, blk=blk)
    seq_blk = lambda r, c: (r // tps, 0, c)
    out, new_hist, h_last = pl.pallas_call(
        kern,
        grid=(m // tm, n_c),
        in_specs=[
            pl.BlockSpec((tm, d), lambda r, c: (r, 0)),
            pl.BlockSpec((nseg, conv_width - 1, tc), seq_blk),
            pl.BlockSpec((nseg, 1, tc), seq_blk),
            pl.BlockSpec((1, d), lambda r, c: (0, 0)),
            pl.BlockSpec((d, tc), lambda r, c: (0, c)),
            pl.BlockSpec((d, tc), lambda r, c: (0, n_c + c)),
            pl.BlockSpec((conv_width, tc), lambda r, c: (0, c)),
            pl.BlockSpec((1, tc), lambda r, c: (0, c)),
            pl.BlockSpec((tc // blk, blk, 2 * blk), lambda r, c: (c, 0, 0)),
            pl.BlockSpec((tc // blk, 1, 2 * blk), lambda r, c: (c, 0, 0)),
            pl.BlockSpec((1, tc), lambda r, c: (0, c)),
            pl.BlockSpec((tc, d), lambda r, c: (c, 0)),
        ],
        out_specs=[
            pl.BlockSpec((tm, d), lambda r, c: (r, 0)),
            pl.BlockSpec((nseg, conv_width - 1, tc), lambda r, c: (r, 0, c)),
            pl.BlockSpec((nseg, 1, tc), lambda r, c: (r, 0, c)),
        ],
        out_shape=[
            jax.ShapeDtypeStruct((m, d), F32),
            jax.ShapeDtypeStruct((n_seq * tps, conv_width - 1, c_rnn), F32),
            jax.ShapeDtypeStruct((n_seq * tps, 1, c_rnn), F32),
        ],
        scratch_shapes=[
            pltpu.VMEM((tm, d), BF16),
            pltpu.VMEM((tm, d), F32),
            pltpu.VMEM((nseg, SUBLANES + tseg, tc), F32),
            pltpu.VMEM((tm, tc), F32),
            pltpu.VMEM((tm, tc), F32),
            pltpu.VMEM((tm, tc), F32),
            pltpu.VMEM((tm, tc), F32),
            pltpu.VMEM((n_c, conv_width - 1, tc), F32),
            pltpu.VMEM((n_c, 1, tc), F32),
        ],
        compiler_params=_params(2),
        name="rg_layer",
    )(x, hist, h0, norm_g.reshape(1, d), w_in, w_in, conv_w, conv_b.reshape(1, c_rnn),
      gate_w, gate_b.reshape(RG_BLOCKS, 1, 2 * blk), log_lam.reshape(1, c_rnn), w_out)
    return out, _last_tile(new_hist, n_seq, tps), _last_tile(h_last, n_seq, tps)


def _ffn_kernel(x_ref, hist_ref, g_ref, wg_ref, wv_ref, cwg_ref, cwv_ref, cbg_ref, cbv_ref, wd_ref,
                fg_ref, out_ref, nh_ref,
                xn_scr, acc_scr, ubuf, hid_scr, ccar,
                *, tseg, nseg, tiles_per_seq, final_norm):
    r = pl.program_id(0)
    c = pl.program_id(1)
    n_c = pl.num_programs(1)
    conv_w = cwg_ref.shape[0]
    tb = r % tiles_per_seq
    pad = SUBLANES - (conv_w - 1)

    @pl.when(c == 0)
    def _():
        xn_scr[...] = _rmsnorm_rows(x_ref[...], g_ref[...]).astype(BF16)
        acc_scr[...] = jnp.zeros_like(acc_scr)

    xn = xn_scr[...]
    halves = (
        (jnp.dot(xn, wg_ref[...], preferred_element_type=F32), cwg_ref, cbg_ref),
        (jnp.dot(xn, wv_ref[...], preferred_element_type=F32), cwv_ref, cbv_ref),
    )

    for s in range(nseg):
        conv = []
        for half, (u, cw_ref, cb_ref) in enumerate(halves):
            ubuf[half, s, SUBLANES:SUBLANES + tseg, :] = u[s * tseg:(s + 1) * tseg]

            def load_state(half=half, s=s):
                for k in range(conv_w - 1):
                    ubuf[half, s, pad + k:pad + k + 1, :] = hist_ref[s, k, half:half + 1, :]

            if tiles_per_seq == 1:
                load_state()
            else:
                pl.when(tb == 0)(load_state)

                @pl.when(tb != 0)
                def _():
                    ubuf[half, s, pad:SUBLANES, :] = ccar[c, half]
            cv = cb_ref[...] + ubuf[half, s, pad:pad + tseg, :] * cw_ref[0:1, :]
            for k in range(1, conv_w):
                cv = cv + ubuf[half, s, pad + k:pad + k + tseg, :] * cw_ref[k:k + 1, :]
            conv.append(cv)
            for k in range(conv_w - 1):
                nh_ref[s, k, half:half + 1, :] = ubuf[half, s, pad + tseg + k:pad + tseg + k + 1, :]
            if tiles_per_seq > 1:
                ccar[c, half] = ubuf[half, s, pad + tseg:SUBLANES + tseg, :]
        hid_scr[s * tseg:(s + 1) * tseg, :] = (jax.nn.gelu(conv[0]) * conv[1]).astype(BF16)

    acc_scr[...] += jnp.dot(hid_scr[...], wd_ref[...], preferred_element_type=F32)

    @pl.when(c == n_c - 1)
    def _():
        res = x_ref[...] + acc_scr[...]
        if final_norm:
            res = _rmsnorm_rows(res, fg_ref[...])
        out_ref[...] = res


def _ffn_layer(x, hist, t, norm_g, w_up, conv_w, conv_b, w_down, final_g):
    m, d = x.shape
    n_seq = m // t
    d_ff = w_down.shape[0]
    conv_width = conv_w.shape[0]
    tf = FF_COL_TILE
    assert d_ff % tf == 0
    n_c = d_ff // tf
    tseg, nseg, tps = _row_tiling(n_seq, t)
    tm = tseg * nseg
    final_norm = final_g is not None
    if final_g is None:
        final_g = jnp.ones((d,), F32)
    kern = functools.partial(_ffn_kernel, tseg=tseg, nseg=nseg, tiles_per_seq=tps,
                             final_norm=final_norm)
    hist4 = hist.reshape(n_seq, conv_width - 1, 2, d_ff)
    conv_b2 = conv_b.reshape(1, 2 * d_ff)
    seq_blk = lambda r, c: (r // tps, 0, 0, c)
    out, new_hist = pl.pallas_call(
        kern,
        grid=(m // tm, n_c),
        in_specs=[
            pl.BlockSpec((tm, d), lambda r, c: (r, 0)),
            pl.BlockSpec((nseg, conv_width - 1, 2, tf), seq_blk),
            pl.BlockSpec((1, d), lambda r, c: (0, 0)),
            pl.BlockSpec((d, tf), lambda r, c: (0, c)),
            pl.BlockSpec((d, tf), lambda r, c: (0, n_c + c)),
            pl.BlockSpec((conv_width, tf), lambda r, c: (0, c)),
            pl.BlockSpec((conv_width, tf), lambda r, c: (0, n_c + c)),
            pl.BlockSpec((1, tf), lambda r, c: (0, c)),
            pl.BlockSpec((1, tf), lambda r, c: (0, n_c + c)),
            pl.BlockSpec((tf, d), lambda r, c: (c, 0)),
            pl.BlockSpec((1, d), lambda r, c: (0, 0)),
        ],
        out_specs=[
            pl.BlockSpec((tm, d), lambda r, c: (r, 0)),
            pl.BlockSpec((nseg, conv_width - 1, 2, tf), lambda r, c: (r, 0, 0, c)),
        ],
        out_shape=[
            jax.ShapeDtypeStruct((m, d), F32),
            jax.ShapeDtypeStruct((n_seq * tps, conv_width - 1, 2, d_ff), F32),
        ],
        scratch_shapes=[
            pltpu.VMEM((tm, d), BF16),
            pltpu.VMEM((tm, d), F32),
            pltpu.VMEM((2, nseg, SUBLANES + tseg, tf), F32),
            pltpu.VMEM((tm, tf), BF16),
            pltpu.VMEM((n_c, 2, conv_width - 1, tf), F32),
        ],
        compiler_params=_params(2),
        name="conv_ffn",
    )(x, hist4, norm_g.reshape(1, d), w_up, w_up, conv_w, conv_w, conv_b2, conv_b2, w_down,
      final_g.reshape(1, d))
    return out, _last_tile(new_hist, n_seq, tps).reshape(n_seq, conv_width - 1, 2 * d_ff)


def _qkv_kernel(x_ref, g_ref, wq_ref, wk_ref, wv_ref, q_ref, k_ref, v_ref, kb_ref, vb_ref, xn_scr,
                *, scale):
    @pl.when(pl.program_id(1) == 0)
    def _():
        xn_scr[...] = _rmsnorm_rows(x_ref[...], g_ref[...]).astype(BF16)

    xn = xn_scr[...]
    q = jnp.dot(xn, wq_ref[...], preferred_element_type=F32)
    q_ref[...] = (q * scale).astype(BF16)
    k = jnp.dot(xn, wk_ref[...], preferred_element_type=F32)
    k_ref[...] = k
    kb_ref[...] = k.astype(BF16)
    v = jnp.dot(xn, wv_ref[...], preferred_element_type=F32)
    v_ref[...] = v
    vb_ref[...] = v.astype(BF16)


def _qkv_proj(x, norm_g, w_qkv, scale):
    m, d = x.shape
    d_att = w_qkv.shape[1] // 3
    tm = min(ROW_TILE, m)
    tn = PROJ_COL_TILE
    n_c = d_att // tn
    tile = pl.BlockSpec((tm, tn), lambda r, c: (r, c))
    return pl.pallas_call(
        functools.partial(_qkv_kernel, scale=scale),
        grid=(m // tm, n_c),
        in_specs=[
            pl.BlockSpec((tm, d), lambda r, c: (r, 0)),
            pl.BlockSpec((1, d), lambda r, c: (0, 0)),
            pl.BlockSpec((d, tn), lambda r, c: (0, c)),
            pl.BlockSpec((d, tn), lambda r, c: (0, n_c + c)),
            pl.BlockSpec((d, tn), lambda r, c: (0, 2 * n_c + c)),
        ],
        out_specs=[tile] * 5,
        out_shape=[
            jax.ShapeDtypeStruct((m, d_att), BF16),
            jax.ShapeDtypeStruct((m, d_att), F32),
            jax.ShapeDtypeStruct((m, d_att), F32),
            jax.ShapeDtypeStruct((m, d_att), BF16),
            jax.ShapeDtypeStruct((m, d_att), BF16),
        ],
        scratch_shapes=[pltpu.VMEM((tm, d), BF16)],
        compiler_params=_params(2),
        name="qkv_proj",
    )(x, norm_g.reshape(1, d), w_qkv, w_qkv, w_qkv)


def _diff_lambda(lam_ref, lambda_init):
    lam = lam_ref[...]
    s1 = jnp.sum(lam[0:1] * lam[1:2], axis=-1, keepdims=True)
    s2 = jnp.sum(lam[2:3] * lam[3:4], axis=-1, keepdims=True)
    return jnp.exp(s1) - jnp.exp(s2) + lambda_init


def _head_norm(o, subln_ref, lambda_init):
    return _rmsnorm_rows(o, subln_ref[...]) * (1.0 - lambda_init)


def _nt_dot(a, b):
    return lax.dot_general(a, b, (((1,), (1,)), ((), ())), preferred_element_type=F32)


def _attn_prompt_kernel(q_ref, k_ref, v_ref, lam_ref, subln_ref, o_ref, m_scr, l_scr, acc_scr,
                        *, lambda_init):
    i = pl.program_id(2)
    tq = q_ref.shape[0]
    hd = q_ref.shape[1] // 2
    q = q_ref[...]

    off = pl.multiple_of(i * tq, tq)
    k_blk = k_ref[pl.ds(off, tq), :]
    v_blk = v_ref[pl.ds(off, tq), :]
    q_chunk = lax.broadcasted_iota(jnp.int32, (tq, tq), 0) // CHUNK
    k_chunk = lax.broadcasted_iota(jnp.int32, (tq, tq), 1) // CHUNK
    visible = k_chunk <= q_chunk
    for j in range(2):
        cols = slice(j * hd, (j + 1) * hd)
        s = jnp.where(visible, _nt_dot(q[:, cols], k_blk[:, cols]), NEG_INF)
        m = jnp.max(s, axis=-1, keepdims=True)
        p = jnp.exp(s - m)
        m_scr[j] = m
        l_scr[j] = jnp.sum(p, axis=-1, keepdims=True)
        acc_scr[j] = jnp.dot(p.astype(BF16), v_blk, preferred_element_type=F32)

    def block(kb, carry):
        off = pl.multiple_of(kb * tq, tq)
        k_blk = k_ref[pl.ds(off, tq), :]
        v_blk = v_ref[pl.ds(off, tq), :]
        for j in range(2):
            cols = slice(j * hd, (j + 1) * hd)
            s = _nt_dot(q[:, cols], k_blk[:, cols])
            m_old = m_scr[j]
            m_new = jnp.maximum(m_old, jnp.max(s, axis=-1, keepdims=True))
            alpha = jnp.exp(m_old - m_new)
            p = jnp.exp(s - m_new)
            m_scr[j] = m_new
            l_scr[j] = alpha * l_scr[j] + jnp.sum(p, axis=-1, keepdims=True)
            acc_scr[j] = alpha * acc_scr[j] + jnp.dot(p.astype(BF16), v_blk,
                                                      preferred_element_type=F32)
        return carry

    lax.fori_loop(0, i, block, 0)

    lam = _diff_lambda(lam_ref, lambda_init)
    o = acc_scr[0] / l_scr[0] - lam * (acc_scr[1] / l_scr[1])
    o_ref[...] = _head_norm(o, subln_ref, lambda_init).astype(BF16)


def _attn_prompt(q, kb, vb, t, lam_p, subln, lambda_init):
    m, d_att = q.shape
    n_seq = m // t
    hw = d_att // N_HEADS
    tq = min(ATTN_TILE, t)
    assert t % tq == 0 and tq % CHUNK == 0
    n_q = t // tq
    return pl.pallas_call(
        functools.partial(_attn_prompt_kernel, lambda_init=lambda_init),
        grid=(n_seq, N_HEADS, n_q),
        in_specs=[
            pl.BlockSpec((tq, hw), lambda b, h, i: (b * n_q + i, h)),
            pl.BlockSpec((t, hw), lambda b, h, i: (b, h)),
            pl.BlockSpec((t, hw), lambda b, h, i: (b, h)),
            pl.BlockSpec(lam_p.shape, lambda b, h, i: (0, 0)),
            pl.BlockSpec((1, hw), lambda b, h, i: (0, 0)),
        ],
        out_specs=pl.BlockSpec((tq, hw), lambda b, h, i: (b * n_q + i, h)),
        out_shape=jax.ShapeDtypeStruct((m, d_att), BF16),
        scratch_shapes=[
            pltpu.VMEM((2, tq, 1), F32),
            pltpu.VMEM((2, tq, 1), F32),
            pltpu.VMEM((2, tq, hw), F32),
        ],
        compiler_params=_params(3),
        name="attn_prompt",
    )(q, kb, vb, lam_p, subln.reshape(1, hw))


def _attn_sample_kernel(q_ref, ck_ref, cv_ref, kn_ref, vn_ref, lam_ref, subln_ref, o_ref,
                        *, lambda_init):
    hd = q_ref.shape[1] // 2
    q = q_ref[...]
    k_cache = ck_ref[0].astype(BF16)
    v_cache = cv_ref[0].astype(BF16)
    k_new = kn_ref[...]
    v_new = vn_ref[...]
    outs = []
    for j in range(2):
        cols = slice(j * hd, (j + 1) * hd)
        s_c = _nt_dot(q[:, cols], k_cache[:, cols])
        s_n = _nt_dot(q[:, cols], k_new[:, cols])
        m = jnp.maximum(jnp.max(s_c, axis=-1, keepdims=True), jnp.max(s_n, axis=-1, keepdims=True))
        p_c = jnp.exp(s_c - m)
        p_n = jnp.exp(s_n - m)
        l = jnp.sum(p_c, axis=-1, keepdims=True) + jnp.sum(p_n, axis=-1, keepdims=True)
        pv = (jnp.dot(p_c.astype(BF16), v_cache, preferred_element_type=F32)
              + jnp.dot(p_n.astype(BF16), v_new, preferred_element_type=F32))
        outs.append(pv / l)
    lam = _diff_lambda(lam_ref, lambda_init)
    o = outs[0] - lam * outs[1]
    o_ref[...] = _head_norm(o, subln_ref, lambda_init).astype(BF16)


def _attn_sample(q, cache_k, cache_v, kb, vb, t, lam_p, subln, lambda_init):
    m, d_att = q.shape
    n_seq = m // t
    past = cache_k.shape[1]
    hw = d_att // N_HEADS
    new_blk = pl.BlockSpec((t, hw), lambda b, h: (b, h))
    cache_blk = pl.BlockSpec((1, past, hw), lambda b, h: (b, 0, h))
    return pl.pallas_call(
        functools.partial(_attn_sample_kernel, lambda_init=lambda_init),
        grid=(n_seq, N_HEADS),
        in_specs=[
            new_blk, cache_blk, cache_blk, new_blk, new_blk,
            pl.BlockSpec(lam_p.shape, lambda b, h: (0, 0)),
            pl.BlockSpec((1, hw), lambda b, h: (0, 0)),
        ],
        out_specs=new_blk,
        out_shape=jax.ShapeDtypeStruct((m, d_att), BF16),
        compiler_params=_params(2),
        name="attn_sample",
    )(q, cache_k, cache_v, kb, vb, lam_p, subln.reshape(1, hw))


def _proj_residual_kernel(o_ref, w_ref, x_ref, out_ref):
    out_ref[...] = x_ref[...] + jnp.dot(o_ref[...], w_ref[...], preferred_element_type=F32)


def _proj_residual(o, w, x):
    m, kdim = o.shape
    n = w.shape[1]
    tm = min(ROW_TILE, m)
    tn = PROJ_COL_TILE
    return pl.pallas_call(
        _proj_residual_kernel,
        grid=(m // tm, n // tn),
        in_specs=[
            pl.BlockSpec((tm, kdim), lambda r, c: (r, 0)),
            pl.BlockSpec((kdim, tn), lambda r, c: (0, c)),
            pl.BlockSpec((tm, tn), lambda r, c: (r, c)),
        ],
        out_specs=pl.BlockSpec((tm, tn), lambda r, c: (r, c)),
        out_shape=jax.ShapeDtypeStruct((m, n), F32),
        compiler_params=_params(2),
        name="attn_out_proj",
    )(o, w, x)


def kernel(x_prompt, x_sample, state_rglru_conv, state_rglru_h, cache_attn_k, cache_attn_v, state_ffn_conv, rg_norm, rg_w_in, rg_conv_w, rg_conv_b, rg_gate_w, rg_gate_b, rg_log_lambda, rg_w_out, at_norm, at_w_qkv, at_lambda, at_subln, at_w_out, ffn_norm, ffn_w_up, ffn_conv_w, ffn_conv_b, ffn_w_down, final_norm):
    bp, tp, d = x_prompt.shape
    bs, ts, _ = x_sample.shape
    depth = ffn_norm.shape[0]
    c_rnn = rg_w_out.shape[1]
    d_att = at_w_out.shape[1]
    head_dim = d_att // (2 * N_HEADS)
    past = cache_attn_k.shape[2]

    xp = x_prompt.reshape(bp * tp, d)
    xs = x_sample.reshape(bs * ts, d)
    p_rg_conv, p_rg_h, p_k, p_v, p_ffn = [], [], [], [], []
    s_rg_conv, s_rg_h, s_k, s_v, s_ffn = [], [], [], [], []
    for layer in range(depth):
        j = layer // 2
        if layer % 2 == 0:
            w = (rg_norm[j], rg_w_in[j].astype(BF16), rg_conv_w[j], rg_conv_b[j],
                 rg_gate_w[j].astype(BF16), rg_gate_b[j], rg_log_lambda[j], rg_w_out[j].astype(BF16))
            zero_hist = jnp.zeros((bp, rg_conv_w.shape[1] - 1, c_rnn), F32)
            zero_h = jnp.zeros((bp, 1, c_rnn), F32)
            xp, cp, hp = _rg_layer(xp, zero_hist, zero_h, tp, *w)
            xs, cs, hs = _rg_layer(xs, state_rglru_conv[j], state_rglru_h[j].reshape(bs, 1, c_rnn),
                                   ts, *w)
            p_rg_conv.append(cp); p_rg_h.append(hp.reshape(bp, c_rnn))
            s_rg_conv.append(cs); s_rg_h.append(hs.reshape(bs, c_rnn))
        else:
            lambda_init = 0.8 - 0.6 * math.exp(-0.3 * layer)
            w_qkv = at_w_qkv[j].astype(BF16)
            w_out = at_w_out[j].astype(BF16)
            scale = head_dim ** -0.5
            qp, kp, vp, kpb, vpb = _qkv_proj(xp, at_norm[j], w_qkv, scale)
            op = _attn_prompt(qp, kpb, vpb, tp, at_lambda[j], at_subln[j], lambda_init)
            xp = _proj_residual(op, w_out, xp)
            qs, kn, vn, knb, vnb = _qkv_proj(xs, at_norm[j], w_qkv, scale)
            os_ = _attn_sample(qs, cache_attn_k[j].reshape(bs, past, d_att),
                               cache_attn_v[j].reshape(bs, past, d_att), knb, vnb, ts,
                               at_lambda[j], at_subln[j], lambda_init)
            xs = _proj_residual(os_, w_out, xs)
            p_k.append(kp.reshape(bp, tp, 2 * N_HEADS, head_dim))
            p_v.append(vp.reshape(bp, tp, N_HEADS, 2 * head_dim))
            s_k.append(kn.reshape(bs, ts, 2 * N_HEADS, head_dim))
            s_v.append(vn.reshape(bs, ts, N_HEADS, 2 * head_dim))
        last = layer == depth - 1
        fw = (ffn_norm[layer], ffn_w_up[layer].astype(BF16), ffn_conv_w[layer], ffn_conv_b[layer],
              ffn_w_down[layer].astype(BF16), final_norm if last else None)
        zero_ffn = jnp.zeros((bp,) + state_ffn_conv.shape[2:], F32)
        xp, fcp = _ffn_layer(xp, zero_ffn, tp, *fw)
        xs, fcs = _ffn_layer(xs, state_ffn_conv[layer], ts, *fw)
        p_ffn.append(fcp); s_ffn.append(fcs)
    stack = lambda arrs: arrs[0][None] if len(arrs) == 1 else jnp.stack(arrs)
    return (xp.reshape(bp, tp, d), xs.reshape(bs, ts, d),
            stack(p_rg_conv), stack(p_rg_h), stack(p_k), stack(p_v), stack(p_ffn),
            stack(s_rg_conv), stack(s_rg_h), stack(s_k), stack(s_v), stack(s_ffn))
```

```python
import functools
import math

import jax
import jax.numpy as jnp
from jax import lax
from jax.experimental import pallas as pl
from jax.experimental.pallas import tpu as pltpu

F32 = jnp.float32
BF16 = jnp.bfloat16

EPS = 1e-6
NEG_INF = -1e30
CHUNK = 64
N_HEADS = 8
RG_BLOCKS = 16
RG_C = 8.0
SUBLANES = 8
VMEM_LIMIT_BYTES = 58 * 1024 * 1024

LAYER_ROW_TILE = 1024
PART_ROWS = 512
PROJ_ROW_TILE = 1024
RG_COL_TILE = 256
FF_COL_TILE = 512
PROJ_COL_TILE = 512
ATTN_TILE = 512


def _params(n_axes):
    return pltpu.CompilerParams(
        dimension_semantics=("arbitrary",) * n_axes,
        vmem_limit_bytes=VMEM_LIMIT_BYTES,
    )


def _rmsnorm_rows(x, g):
    ms = jnp.mean(x * x, axis=-1, keepdims=True)
    return (x * lax.rsqrt(ms + EPS)) * g


def _row_tiling(n_seq, t):
    if t >= LAYER_ROW_TILE:
        assert t % LAYER_ROW_TILE == 0
        return LAYER_ROW_TILE, 1, t // LAYER_ROW_TILE
    assert LAYER_ROW_TILE % t == 0 and t % SUBLANES == 0
    nseg = min(LAYER_ROW_TILE // t, n_seq)
    assert n_seq % nseg == 0
    return t, nseg, 1


def _part_rows(tseg, nseg):
    tm = tseg * nseg
    if nseg == 1 and tm % PART_ROWS == 0:
        return PART_ROWS
    return tm


def _last_tile(state, n_seq, tiles_per_seq):
    if tiles_per_seq == 1:
        return state
    return state.reshape((n_seq, tiles_per_seq) + state.shape[1:])[:, -1]


def _x_tile_copy(x_hbm, x_buf, sem, r):
    tm = x_buf.shape[0]
    return pltpu.make_async_copy(x_hbm.at[pl.ds(pl.multiple_of(r * tm, tm), tm)], x_buf, sem)


def _stream_x_tile(x_hbm, x_buf, sem, consume):
    r = pl.program_id(0)
    c = pl.program_id(1)
    n_r = pl.num_programs(0)

    @pl.when(c == 0)
    def _():
        @pl.when(r == 0)
        def _():
            _x_tile_copy(x_hbm, x_buf, sem, r).start()

        _x_tile_copy(x_hbm, x_buf, sem, r).wait()
        consume()

    @pl.when(jnp.logical_and(c == 1, r + 1 < n_r))
    def _():
        _x_tile_copy(x_hbm, x_buf, sem, r + 1).start()


def _rg_kernel(x_hbm, hist_ref, h0_ref, g_ref, wg_ref, wr_ref, cw_ref, cb_ref, gw_ref, gb_ref,
               ll_ref, wo_ref,
               out_ref, nh_ref, hl_ref,
               x_buf, x_sem, xn_scr, rbuf, xc_scr, a_scr, b_scr, h_scr, ccar, hcar,
               *, tseg, nseg, tiles_per_seq, blk, part):
    r = pl.program_id(0)
    c = pl.program_id(1)
    conv_w = cw_ref.shape[0]
    tc = wg_ref.shape[1]
    tm = tseg * nseg
    tb = r % tiles_per_seq
    pad = SUBLANES - (conv_w - 1)

    def consume():
        x = x_buf[...]
        xn_scr[...] = _rmsnorm_rows(x, g_ref[...]).astype(BF16)
        out_ref[...] = x

    _stream_x_tile(x_hbm, x_buf, x_sem, consume)

    for s in range(nseg):
        if tiles_per_seq == 1:
            rbuf[s, pad:SUBLANES, :] = hist_ref[s]
        else:
            @pl.when(tb == 0)
            def _():
                rbuf[s, pad:SUBLANES, :] = hist_ref[s]
                hcar[c] = h0_ref[s]

            @pl.when(tb != 0)
            def _():
                rbuf[s, pad:SUBLANES, :] = ccar[c]

    z = -ll_ref[...]
    softplus = jnp.maximum(z, 0.0) + jnp.log1p(jnp.exp(-jnp.abs(z)))

    n_parts = tm // part
    gelu_gate = []
    for p in range(n_parts):
        xn = xn_scr[p * part:(p + 1) * part, :]
        gelu_gate.append(jax.nn.gelu(jnp.dot(xn, wg_ref[...], preferred_element_type=F32)))
        rec = jnp.dot(xn, wr_ref[...], preferred_element_type=F32)
        if nseg == 1:
            rbuf[0, SUBLANES + p * part:SUBLANES + (p + 1) * part, :] = rec
        else:
            for s in range(nseg):
                rbuf[s, SUBLANES:SUBLANES + tseg, :] = rec[s * tseg:(s + 1) * tseg]

    h_run = None
    for p in range(n_parts):
        rows = slice(p * part, (p + 1) * part)

        segs = [(0, p * part, part)] if nseg == 1 else [(s, 0, tseg) for s in range(nseg)]
        for s, start, length in segs:
            xc = cb_ref[...] + rbuf[s, pad + start:pad + start + length, :] * cw_ref[0:1, :]
            for k in range(1, conv_w):
                lo = pad + start + k
                xc = xc + rbuf[s, lo:lo + length, :] * cw_ref[k:k + 1, :]
            lo = start if nseg == 1 else s * tseg
            xc_scr[lo:lo + length, :] = xc

        for n in range(tc // blk):
            cols = slice(n * blk, (n + 1) * blk)
            xc_n = xc_scr[rows, cols]
            gate = jnp.dot(xc_n.astype(BF16), gw_ref[n], preferred_element_type=F32) + gb_ref[n]
            gate = jax.nn.sigmoid(gate)
            log_a = (-RG_C * gate[:, :blk]) * softplus[:, cols]
            a_n = jnp.exp(log_a)
            a_scr[rows, cols] = a_n
            b_scr[rows, cols] = jnp.sqrt(1.0 - a_n * a_n) * gate[:, blk:] * xc_n

        a = a_scr[rows, :].reshape(part // SUBLANES, SUBLANES, tc)
        b = b_scr[rows, :].reshape(part // SUBLANES, SUBLANES, tc)
        row = lax.broadcasted_iota(jnp.int32, a.shape, 1)
        shift = 1
        while shift < SUBLANES:
            keep = row >= shift
            a_prev = pltpu.roll(a, shift, axis=1)
            b_prev = pltpu.roll(b, shift, axis=1)
            b = jnp.where(keep, a * b_prev + b, b)
            a = jnp.where(keep, a * a_prev, a)
            shift *= 2
        a_scr[rows, :] = a.reshape(part, tc)
        b_scr[rows, :] = b.reshape(part, tc)

        for s, start, length in segs:
            lo = start if nseg == 1 else s * tseg
            if nseg > 1 or p == 0:
                h_run = h0_ref[s] if tiles_per_seq == 1 else hcar[c]

            def group(i, h_prev, lo=lo):
                off = pl.multiple_of(lo + i * SUBLANES, SUBLANES)
                h = a_scr[pl.ds(off, SUBLANES), :] * h_prev + b_scr[pl.ds(off, SUBLANES), :]
                h_scr[pl.ds(off, SUBLANES), :] = h
                return h[SUBLANES - 1:SUBLANES, :]

            h_run = lax.fori_loop(0, length // SUBLANES, group, h_run, unroll=8)
            if nseg > 1 or p == n_parts - 1:
                hl_ref[s] = h_run
                if tiles_per_seq > 1:
                    hcar[c] = h_run

        out_ref[rows, :] += jnp.dot((h_scr[rows, :] * gelu_gate[p]).astype(BF16), wo_ref[...],
                                    preferred_element_type=F32)

    for s in range(nseg):
        new_hist = rbuf[s, pad + tseg:SUBLANES + tseg, :]
        nh_ref[s] = new_hist
        if tiles_per_seq > 1:
            ccar[c] = new_hist


def _rg_layer(x, hist, h0, t, norm_g, w_in, conv_w, conv_b, gate_w, gate_b, log_lam, w_out):
    m, d = x.shape
    n_seq = m // t
    c_rnn = w_out.shape[0]
    conv_width = conv_w.shape[0]
    blk = c_rnn // RG_BLOCKS
    tc = min(RG_COL_TILE, c_rnn)
    n_c = c_rnn // tc
    assert n_c >= 2
    tseg, nseg, tps = _row_tiling(n_seq, t)
    tm = tseg * nseg
    kern = functools.partial(_rg_kernel, tseg=tseg, nseg=nseg, tiles_per_seq=tps, blk=blk,
                             part=_part_rows(tseg, nseg))
    seq_blk = lambda r, c: (r // tps, 0, c)
    out, new_hist, h_last = pl.pallas_call(
        kern,
        grid=(m // tm, n_c),
        in_specs=[
            pl.BlockSpec(memory_space=pl.ANY),
            pl.BlockSpec((nseg, conv_width - 1, tc), seq_blk),
            pl.BlockSpec((nseg, 1, tc), seq_blk),
            pl.BlockSpec((1, d), lambda r, c: (0, 0)),
            pl.BlockSpec((d, tc), lambda r, c: (0, c)),
            pl.BlockSpec((d, tc), lambda r, c: (0, n_c + c)),
            pl.BlockSpec((conv_width, tc), lambda r, c: (0, c)),
            pl.BlockSpec((1, tc), lambda r, c: (0, c)),
            pl.BlockSpec((tc // blk, blk, 2 * blk), lambda r, c: (c, 0, 0)),
            pl.BlockSpec((tc // blk, 1, 2 * blk), lambda r, c: (c, 0, 0)),
            pl.BlockSpec((1, tc), lambda r, c: (0, c)),
            pl.BlockSpec((tc, d), lambda r, c: (c, 0)),
        ],
        out_specs=[
            pl.BlockSpec((tm, d), lambda r, c: (r, 0)),
            pl.BlockSpec((nseg, conv_width - 1, tc), lambda r, c: (r, 0, c)),
            pl.BlockSpec((nseg, 1, tc), lambda r, c: (r, 0, c)),
        ],
        out_shape=[
            jax.ShapeDtypeStruct((m, d), F32),
            jax.ShapeDtypeStruct((n_seq * tps, conv_width - 1, c_rnn), F32),
            jax.ShapeDtypeStruct((n_seq * tps, 1, c_rnn), F32),
        ],
        scratch_shapes=[
            pltpu.VMEM((tm, d), F32),
            pltpu.SemaphoreType.DMA,
            pltpu.VMEM((tm, d), BF16),
            pltpu.VMEM((nseg, SUBLANES + tseg, tc), F32),
            pltpu.VMEM((tm, tc), F32),
            pltpu.VMEM((tm, tc), F32),
            pltpu.VMEM((tm, tc), F32),
            pltpu.VMEM((tm, tc), F32),
            pltpu.VMEM((n_c, conv_width - 1, tc), F32),
            pltpu.VMEM((n_c, 1, tc), F32),
        ],
        compiler_params=_params(2),
        name="rg_layer",
    )(x, hist, h0, norm_g.reshape(1, d), w_in, w_in, conv_w, conv_b.reshape(1, c_rnn),
      gate_w, gate_b.reshape(RG_BLOCKS, 1, 2 * blk), log_lam.reshape(1, c_rnn), w_out)
    return out, _last_tile(new_hist, n_seq, tps), _last_tile(h_last, n_seq, tps)


def _ffn_kernel(x_hbm, hist_ref, g_ref, wg_ref, wv_ref, cwg_ref, cwv_ref, cbg_ref, cbv_ref, wd_ref,
                fg_ref, out_ref, nh_ref,
                x_buf, x_sem, xn_scr, ubuf, hid_scr, ccar,
                *, tseg, nseg, tiles_per_seq, final_norm, part):
    r = pl.program_id(0)
    c = pl.program_id(1)
    n_c = pl.num_programs(1)
    conv_w = cwg_ref.shape[0]
    tm = tseg * nseg
    tb = r % tiles_per_seq
    pad = SUBLANES - (conv_w - 1)
    halves = ((wg_ref, cwg_ref, cbg_ref), (wv_ref, cwv_ref, cbv_ref))

    def consume():
        x = x_buf[...]
        xn_scr[...] = _rmsnorm_rows(x, g_ref[...]).astype(BF16)
        out_ref[...] = x

    _stream_x_tile(x_hbm, x_buf, x_sem, consume)

    for s in range(nseg):
        for half in range(2):
            def load_state(half=half, s=s):
                for k in range(conv_w - 1):
                    ubuf[half, s, pad + k:pad + k + 1, :] = hist_ref[s, k, half:half + 1, :]

            if tiles_per_seq == 1:
                load_state()
            else:
                pl.when(tb == 0)(load_state)

                @pl.when(tb != 0)
                def _():
                    ubuf[half, s, pad:SUBLANES, :] = ccar[c, half]

    n_parts = tm // part
    for p in range(n_parts):
        xn = xn_scr[p * part:(p + 1) * part, :]
        for half, (w_ref, _, _) in enumerate(halves):
            u = jnp.dot(xn, w_ref[...], preferred_element_type=F32)
            if nseg == 1:
                ubuf[half, 0, SUBLANES + p * part:SUBLANES + (p + 1) * part, :] = u
            else:
                for s in range(nseg):
                    ubuf[half, s, SUBLANES:SUBLANES + tseg, :] = u[s * tseg:(s + 1) * tseg]

    for p in range(n_parts):
        rows = slice(p * part, (p + 1) * part)
        segs = [(0, p * part, part)] if nseg == 1 else [(s, 0, tseg) for s in range(nseg)]
        for s, start, length in segs:
            conv = []
            for half, (_, cw_ref, cb_ref) in enumerate(halves):
                cv = cb_ref[...] + ubuf[half, s, pad + start:pad + start + length, :] * cw_ref[0:1, :]
                for k in range(1, conv_w):
                    lo = pad + start + k
                    cv = cv + ubuf[half, s, lo:lo + length, :] * cw_ref[k:k + 1, :]
                conv.append(cv)
            hid = (jax.nn.gelu(conv[0]) * conv[1]).astype(BF16)
            if nseg > 1:
                hid_scr[s * tseg:(s + 1) * tseg, :] = hid
        if nseg > 1:
            hid = hid_scr[...]
        out_ref[rows, :] += jnp.dot(hid, wd_ref[...], preferred_element_type=F32)

    for s in range(nseg):
        for half in range(2):
            for k in range(conv_w - 1):
                nh_ref[s, k, half:half + 1, :] = ubuf[half, s, pad + tseg + k:pad + tseg + k + 1, :]
            if tiles_per_seq > 1:
                ccar[c, half] = ubuf[half, s, pad + tseg:SUBLANES + tseg, :]

    if final_norm:
        @pl.when(c == n_c - 1)
        def _():
            out_ref[...] = _rmsnorm_rows(out_ref[...], fg_ref[...])


def _ffn_layer(x, hist, t, norm_g, w_up, conv_w, conv_b, w_down, final_g):
    m, d = x.shape
    n_seq = m // t
    d_ff = w_down.shape[0]
    conv_width = conv_w.shape[0]
    tf = FF_COL_TILE
    assert d_ff % tf == 0
    n_c = d_ff // tf
    assert n_c >= 2
    tseg, nseg, tps = _row_tiling(n_seq, t)
    tm = tseg * nseg
    final_norm = final_g is not None
    if final_g is None:
        final_g = jnp.ones((d,), F32)
    kern = functools.partial(_ffn_kernel, tseg=tseg, nseg=nseg, tiles_per_seq=tps,
                             final_norm=final_norm, part=_part_rows(tseg, nseg))
    hist4 = hist.reshape(n_seq, conv_width - 1, 2, d_ff)
    conv_b2 = conv_b.reshape(1, 2 * d_ff)
    out, new_hist = pl.pallas_call(
        kern,
        grid=(m // tm, n_c),
        in_specs=[
            pl.BlockSpec(memory_space=pl.ANY),
            pl.BlockSpec((nseg, conv_width - 1, 2, tf), lambda r, c: (r // tps, 0, 0, c)),
            pl.BlockSpec((1, d), lambda r, c: (0, 0)),
            pl.BlockSpec((d, tf), lambda r, c: (0, c)),
            pl.BlockSpec((d, tf), lambda r, c: (0, n_c + c)),
            pl.BlockSpec((conv_width, tf), lambda r, c: (0, c)),
            pl.BlockSpec((conv_width, tf), lambda r, c: (0, n_c + c)),
            pl.BlockSpec((1, tf), lambda r, c: (0, c)),
            pl.BlockSpec((1, tf), lambda r, c: (0, n_c + c)),
            pl.BlockSpec((tf, d), lambda r, c: (c, 0)),
            pl.BlockSpec((1, d), lambda r, c: (0, 0)),
        ],
        out_specs=[
            pl.BlockSpec((tm, d), lambda r, c: (r, 0)),
            pl.BlockSpec((nseg, conv_width - 1, 2, tf), lambda r, c: (r, 0, 0, c)),
        ],
        out_shape=[
            jax.ShapeDtypeStruct((m, d), F32),
            jax.ShapeDtypeStruct((n_seq * tps, conv_width - 1, 2, d_ff), F32),
        ],
        scratch_shapes=[
            pltpu.VMEM((tm, d), F32),
            pltpu.SemaphoreType.DMA,
            pltpu.VMEM((tm, d), BF16),
            pltpu.VMEM((2, nseg, SUBLANES + tseg, tf), F32),
            pltpu.VMEM((tm if nseg > 1 else SUBLANES, tf), BF16),
            pltpu.VMEM((n_c, 2, conv_width - 1, tf), F32),
        ],
        compiler_params=_params(2),
        name="conv_ffn",
    )(x, hist4, norm_g.reshape(1, d), w_up, w_up, conv_w, conv_w, conv_b2, conv_b2, w_down,
      final_g.reshape(1, d))
    return out, _last_tile(new_hist, n_seq, tps).reshape(n_seq, conv_width - 1, 2 * d_ff)


def _qkv_kernel(x_ref, g_ref, wq_ref, wk_ref, wv_ref, q_ref, k_ref, v_ref, kb_ref, vb_ref, xn_scr,
                *, scale):
    @pl.when(pl.program_id(1) == 0)
    def _():
        xn_scr[...] = _rmsnorm_rows(x_ref[...], g_ref[...]).astype(BF16)

    xn = xn_scr[...]
    q = jnp.dot(xn, wq_ref[...], preferred_element_type=F32)
    q_ref[...] = (q * scale).astype(BF16)
    k = jnp.dot(xn, wk_ref[...], preferred_element_type=F32)
    k_ref[...] = k
    kb_ref[...] = k.astype(BF16)
    v = jnp.dot(xn, wv_ref[...], preferred_element_type=F32)
    v_ref[...] = v
    vb_ref[...] = v.astype(BF16)


def _qkv_proj(x, norm_g, w_qkv, scale):
    m, d = x.shape
    d_att = w_qkv.shape[1] // 3
    tm = min(PROJ_ROW_TILE, m)
    tn = PROJ_COL_TILE
    n_c = d_att // tn
    tile = pl.BlockSpec((tm, tn), lambda r, c: (r, c))
    return pl.pallas_call(
        functools.partial(_qkv_kernel, scale=scale),
        grid=(m // tm, n_c),
        in_specs=[
            pl.BlockSpec((tm, d), lambda r, c: (r, 0)),
            pl.BlockSpec((1, d), lambda r, c: (0, 0)),
            pl.BlockSpec((d, tn), lambda r, c: (0, c)),
            pl.BlockSpec((d, tn), lambda r, c: (0, n_c + c)),
            pl.BlockSpec((d, tn), lambda r, c: (0, 2 * n_c + c)),
        ],
        out_specs=[tile] * 5,
        out_shape=[
            jax.ShapeDtypeStruct((m, d_att), BF16),
            jax.ShapeDtypeStruct((m, d_att), F32),
            jax.ShapeDtypeStruct((m, d_att), F32),
            jax.ShapeDtypeStruct((m, d_att), BF16),
            jax.ShapeDtypeStruct((m, d_att), BF16),
        ],
        scratch_shapes=[pltpu.VMEM((tm, d), BF16)],
        compiler_params=_params(2),
        name="qkv_proj",
    )(x, norm_g.reshape(1, d), w_qkv, w_qkv, w_qkv)


def _diff_lambda(lam_ref, lambda_init):
    lam = lam_ref[...]
    s1 = jnp.sum(lam[0:1] * lam[1:2], axis=-1, keepdims=True)
    s2 = jnp.sum(lam[2:3] * lam[3:4], axis=-1, keepdims=True)
    return jnp.exp(s1) - jnp.exp(s2) + lambda_init


def _head_norm(o, subln_ref, lambda_init):
    return _rmsnorm_rows(o, subln_ref[...]) * (1.0 - lambda_init)


def _nt_dot(a, b):
    return lax.dot_general(a, b, (((1,), (1,)), ((), ())), preferred_element_type=F32)


def _attn_prompt_kernel(q_ref, k_ref, v_ref, lam_ref, subln_ref, o_ref, m_scr, l_scr, acc_scr,
                        *, lambda_init):
    i = pl.program_id(2)
    tq = q_ref.shape[0]
    hd = q_ref.shape[1] // 2
    q = q_ref[...]

    off = pl.multiple_of(i * tq, tq)
    k_blk = k_ref[pl.ds(off, tq), :]
    v_blk = v_ref[pl.ds(off, tq), :]
    q_chunk = lax.broadcasted_iota(jnp.int32, (tq, tq), 0) // CHUNK
    k_chunk = lax.broadcasted_iota(jnp.int32, (tq, tq), 1) // CHUNK
    visible = k_chunk <= q_chunk
    for j in range(2):
        cols = slice(j * hd, (j + 1) * hd)
        s = jnp.where(visible, _nt_dot(q[:, cols], k_blk[:, cols]), NEG_INF)
        m = jnp.max(s, axis=-1, keepdims=True)
        p = jnp.exp(s - m)
        m_scr[j] = m
        l_scr[j] = jnp.sum(p, axis=-1, keepdims=True)
        acc_scr[j] = jnp.dot(p.astype(BF16), v_blk, preferred_element_type=F32)

    def block(kb, carry):
        off = pl.multiple_of(kb * tq, tq)
        k_blk = k_ref[pl.ds(off, tq), :]
        v_blk = v_ref[pl.ds(off, tq), :]
        for j in range(2):
            cols = slice(j * hd, (j + 1) * hd)
            s = _nt_dot(q[:, cols], k_blk[:, cols])
            m_old = m_scr[j]
            m_new = jnp.maximum(m_old, jnp.max(s, axis=-1, keepdims=True))
            alpha = jnp.exp(m_old - m_new)
            p = jnp.exp(s - m_new)
            m_scr[j] = m_new
            l_scr[j] = alpha * l_scr[j] + jnp.sum(p, axis=-1, keepdims=True)
            acc_scr[j] = alpha * acc_scr[j] + jnp.dot(p.astype(BF16), v_blk,
                                                      preferred_element_type=F32)
        return carry

    lax.fori_loop(0, i, block, 0)

    lam = _diff_lambda(lam_ref, lambda_init)
    o = acc_scr[0] / l_scr[0] - lam * (acc_scr[1] / l_scr[1])
    o_ref[...] = _head_norm(o, subln_ref, lambda_init).astype(BF16)


def _attn_prompt(q, kb, vb, t, lam_p, subln, lambda_init):
    m, d_att = q.shape
    n_seq = m // t
    hw = d_att // N_HEADS
    tq = min(ATTN_TILE, t)
    assert t % tq == 0 and tq % CHUNK == 0
    n_q = t // tq
    return pl.pallas_call(
        functools.partial(_attn_prompt_kernel, lambda_init=lambda_init),
        grid=(n_seq, N_HEADS, n_q),
        in_specs=[
            pl.BlockSpec((tq, hw), lambda b, h, i: (b * n_q + i, h)),
            pl.BlockSpec((t, hw), lambda b, h, i: (b, h)),
            pl.BlockSpec((t, hw), lambda b, h, i: (b, h)),
            pl.BlockSpec(lam_p.shape, lambda b, h, i: (0, 0)),
            pl.BlockSpec((1, hw), lambda b, h, i: (0, 0)),
        ],
        out_specs=pl.BlockSpec((tq, hw), lambda b, h, i: (b * n_q + i, h)),
        out_shape=jax.ShapeDtypeStruct((m, d_att), BF16),
        scratch_shapes=[
            pltpu.VMEM((2, tq, 1), F32),
            pltpu.VMEM((2, tq, 1), F32),
            pltpu.VMEM((2, tq, hw), F32),
        ],
        compiler_params=_params(3),
        name="attn_prompt",
    )(q, kb, vb, lam_p, subln.reshape(1, hw))


def _attn_sample_kernel(q_ref, ck_ref, cv_ref, kn_ref, vn_ref, lam_ref, subln_ref, o_ref,
                        *, lambda_init):
    hd = q_ref.shape[1] // 2
    q = q_ref[...]
    k_cache = ck_ref[0].astype(BF16)
    v_cache = cv_ref[0].astype(BF16)
    k_new = kn_ref[...]
    v_new = vn_ref[...]
    outs = []
    for j in range(2):
        cols = slice(j * hd, (j + 1) * hd)
        s_c = _nt_dot(q[:, cols], k_cache[:, cols])
        s_n = _nt_dot(q[:, cols], k_new[:, cols])
        m = jnp.maximum(jnp.max(s_c, axis=-1, keepdims=True), jnp.max(s_n, axis=-1, keepdims=True))
        p_c = jnp.exp(s_c - m)
        p_n = jnp.exp(s_n - m)
        l = jnp.sum(p_c, axis=-1, keepdims=True) + jnp.sum(p_n, axis=-1, keepdims=True)
        pv = (jnp.dot(p_c.astype(BF16), v_cache, preferred_element_type=F32)
              + jnp.dot(p_n.astype(BF16), v_new, preferred_element_type=F32))
        outs.append(pv / l)
    lam = _diff_lambda(lam_ref, lambda_init)
    o = outs[0] - lam * outs[1]
    o_ref[...] = _head_norm(o, subln_ref, lambda_init).astype(BF16)


def _attn_sample(q, cache_k, cache_v, kb, vb, t, lam_p, subln, lambda_init):
    m, d_att = q.shape
    n_seq = m // t
    past = cache_k.shape[1]
    hw = d_att // N_HEADS
    new_blk = pl.BlockSpec((t, hw), lambda b, h: (b, h))
    cache_blk = pl.BlockSpec((1, past, hw), lambda b, h: (b, 0, h))
    return pl.pallas_call(
        functools.partial(_attn_sample_kernel, lambda_init=lambda_init),
        grid=(n_seq, N_HEADS),
        in_specs=[
            new_blk, cache_blk, cache_blk, new_blk, new_blk,
            pl.BlockSpec(lam_p.shape, lambda b, h: (0, 0)),
            pl.BlockSpec((1, hw), lambda b, h: (0, 0)),
        ],
        out_specs=new_blk,
        out_shape=jax.ShapeDtypeStruct((m, d_att), BF16),
        compiler_params=_params(2),
        name="attn_sample",
    )(q, cache_k, cache_v, kb, vb, lam_p, subln.reshape(1, hw))


def _proj_residual_kernel(o_ref, w_ref, x_ref, out_ref):
    out_ref[...] = x_ref[...] + jnp.dot(o_ref[...], w_ref[...], preferred_element_type=F32)


def _proj_residual(o, w, x):
    m, kdim = o.shape
    n = w.shape[1]
    tm = min(PROJ_ROW_TILE, m)
    tn = PROJ_COL_TILE
    return pl.pallas_call(
        _proj_residual_kernel,
        grid=(m // tm, n // tn),
        in_specs=[
            pl.BlockSpec((tm, kdim), lambda r, c: (r, 0)),
            pl.BlockSpec((kdim, tn), lambda r, c: (0, c)),
            pl.BlockSpec((tm, tn), lambda r, c: (r, c)),
        ],
        out_specs=pl.BlockSpec((tm, tn), lambda r, c: (r, c)),
        out_shape=jax.ShapeDtypeStruct((m, n), F32),
        compiler_params=_params(2),
        name="attn_out_proj",
    )(o, w, x)


def kernel(x_prompt, x_sample, state_rglru_conv, state_rglru_h, cache_attn_k, cache_attn_v, state_ffn_conv, rg_norm, rg_w_in, rg_conv_w, rg_conv_b, rg_gate_w, rg_gate_b, rg_log_lambda, rg_w_out, at_norm, at_w_qkv, at_lambda, at_subln, at_w_out, ffn_norm, ffn_w_up, ffn_conv_w, ffn_conv_b, ffn_w_down, final_norm):
    bp, tp, d = x_prompt.shape
    bs, ts, _ = x_sample.shape
    depth = ffn_norm.shape[0]
    c_rnn = rg_w_out.shape[1]
    d_att = at_w_out.shape[1]
    head_dim = d_att // (2 * N_HEADS)
    past = cache_attn_k.shape[2]

    xp = x_prompt.reshape(bp * tp, d)
    xs = x_sample.reshape(bs * ts, d)
    p_rg_conv, p_rg_h, p_k, p_v, p_ffn = [], [], [], [], []
    s_rg_conv, s_rg_h, s_k, s_v, s_ffn = [], [], [], [], []
    for layer in range(depth):
        j = layer // 2
        if layer % 2 == 0:
            w = (rg_norm[j], rg_w_in[j].astype(BF16), rg_conv_w[j], rg_conv_b[j],
                 rg_gate_w[j].astype(BF16), rg_gate_b[j], rg_log_lambda[j], rg_w_out[j].astype(BF16))
            zero_hist = jnp.zeros((bp, rg_conv_w.shape[1] - 1, c_rnn), F32)
            zero_h = jnp.zeros((bp, 1, c_rnn), F32)
            xp, cp, hp = _rg_layer(xp, zero_hist, zero_h, tp, *w)
            xs, cs, hs = _rg_layer(xs, state_rglru_conv[j], state_rglru_h[j].reshape(bs, 1, c_rnn),
                                   ts, *w)
            p_rg_conv.append(cp); p_rg_h.append(hp.reshape(bp, c_rnn))
            s_rg_conv.append(cs); s_rg_h.append(hs.reshape(bs, c_rnn))
        else:
            lambda_init = 0.8 - 0.6 * math.exp(-0.3 * layer)
            w_qkv = at_w_qkv[j].astype(BF16)
            w_out = at_w_out[j].astype(BF16)
            scale = head_dim ** -0.5
            qp, kp, vp, kpb, vpb = _qkv_proj(xp, at_norm[j], w_qkv, scale)
            op = _attn_prompt(qp, kpb, vpb, tp, at_lambda[j], at_subln[j], lambda_init)
            xp = _proj_residual(op, w_out, xp)
            qs, kn, vn, knb, vnb = _qkv_proj(xs, at_norm[j], w_qkv, scale)
            os_ = _attn_sample(qs, cache_attn_k[j].reshape(bs, past, d_att),
                               cache_attn_v[j].reshape(bs, past, d_att), knb, vnb, ts,
                               at_lambda[j], at_subln[j], lambda_init)
            xs = _proj_residual(os_, w_out, xs)
            p_k.append(kp.reshape(bp, tp, 2 * N_HEADS, head_dim))
            p_v.append(vp.reshape(bp, tp, N_HEADS, 2 * head_dim))
            s_k.append(kn.reshape(bs, ts, 2 * N_HEADS, head_dim))
            s_v.append(vn.reshape(bs, ts, N_HEADS, 2 * head_dim))
        last = layer == depth - 1
        fw = (ffn_norm[layer], ffn_w_up[layer].astype(BF16), ffn_conv_w[layer], ffn_conv_b[layer],
              ffn_w_down[layer].astype(BF16), final_norm if last else None)
        zero_ffn = jnp.zeros((bp,) + state_ffn_conv.shape[2:], F32)
        xp, fcp = _ffn_layer(xp, zero_ffn, tp, *fw)
        xs, fcs = _ffn_layer(xs, state_ffn_conv[layer], ts, *fw)
        p_ffn.append(fcp); s_ffn.append(fcs)
    stack = lambda arrs: arrs[0][None] if len(arrs) == 1 else jnp.stack(arrs)
    return (xp.reshape(bp, tp, d), xs.reshape(bs, ts, d),
            stack(p_rg_conv), stack(p_rg_h), stack(p_k), stack(p_v), stack(p_ffn),
            stack(s_rg_conv), stack(s_rg_h), stack(s_k), stack(s_v), stack(s_ffn))
```

```python
import functools
import math

import jax
import jax.numpy as jnp
from jax import lax
from jax.experimental import pallas as pl
from jax.experimental.pallas import tpu as pltpu

F32 = jnp.float32
BF16 = jnp.bfloat16

EPS = 1e-6
NEG_INF = -1e30
CHUNK = 64
N_HEADS = 8
RG_BLOCKS = 16
RG_C = 8.0
SUBLANES = 8
LANES = 128
VMEM_LIMIT_BYTES = 58 * 1024 * 1024

LAYER_ROW_TILE = 1024
PART_ROWS = 512
PROJ_ROW_TILE = 1024
QKV_ROW_TILE = 512
RG_COL_TILE = 256
FF_COL_TILE = 512
PROJ_COL_TILE = 512
ATTN_TILE = 512


def _params(n_axes):
    return pltpu.CompilerParams(
        dimension_semantics=("arbitrary",) * n_axes,
        vmem_limit_bytes=VMEM_LIMIT_BYTES,
    )


def _rmsnorm_rows(x, g):
    ms = jnp.mean(x * x, axis=-1, keepdims=True)
    return (x * lax.rsqrt(ms + EPS)) * g


def _row_tiling(n_seq, t):
    if t >= LAYER_ROW_TILE:
        assert t % LAYER_ROW_TILE == 0
        return LAYER_ROW_TILE, 1, t // LAYER_ROW_TILE
    assert LAYER_ROW_TILE % t == 0 and t % SUBLANES == 0
    nseg = min(LAYER_ROW_TILE // t, n_seq)
    assert n_seq % nseg == 0
    return t, nseg, 1


def _part_rows(tseg, nseg):
    tm = tseg * nseg
    if nseg == 1 and tm % PART_ROWS == 0:
        return PART_ROWS
    return tm


def _last_tile(state, n_seq, tiles_per_seq):
    if tiles_per_seq == 1:
        return state
    return state.reshape((n_seq, tiles_per_seq) + state.shape[1:])[:, -1]


def _x_tile_copy(x_hbm, x_buf, sem, r):
    tm = x_buf.shape[0]
    return pltpu.make_async_copy(x_hbm.at[pl.ds(pl.multiple_of(r * tm, tm), tm)], x_buf, sem)


def _stream_x_tile(x_hbm, x_buf, sem, consume):
    r = pl.program_id(0)
    c = pl.program_id(1)
    n_r = pl.num_programs(0)

    @pl.when(c == 0)
    def _():
        @pl.when(r == 0)
        def _():
            _x_tile_copy(x_hbm, x_buf, sem, r).start()

        _x_tile_copy(x_hbm, x_buf, sem, r).wait()
        consume()

    @pl.when(jnp.logical_and(c == 1, r + 1 < n_r))
    def _():
        _x_tile_copy(x_hbm, x_buf, sem, r + 1).start()


def _rg_kernel(x_hbm, hist_ref, h0_ref, g_ref, wg_ref, wr_ref, cw_ref, cb_ref, gw_ref, gb_ref,
               ll_ref, wo_ref,
               out_ref, nh_ref, hl_ref,
               x_buf, x_sem, xn_scr, rbuf, xc_scr, a_scr, b_scr, h_scr, ccar, hcar,
               *, tseg, nseg, tiles_per_seq, blk, part):
    r = pl.program_id(0)
    c = pl.program_id(1)
    conv_w = cw_ref.shape[0]
    tc = wg_ref.shape[1]
    tm = tseg * nseg
    tb = r % tiles_per_seq
    pad = SUBLANES - (conv_w - 1)

    def consume():
        x = x_buf[...]
        xn_scr[...] = _rmsnorm_rows(x, g_ref[...]).astype(BF16)
        out_ref[...] = x

    _stream_x_tile(x_hbm, x_buf, x_sem, consume)

    for s in range(nseg):
        if tiles_per_seq == 1:
            rbuf[s, pad:SUBLANES, :] = hist_ref[s]
        else:
            @pl.when(tb == 0)
            def _():
                rbuf[s, pad:SUBLANES, :] = hist_ref[s]
                hcar[c] = h0_ref[s]

            @pl.when(tb != 0)
            def _():
                rbuf[s, pad:SUBLANES, :] = ccar[c]

    z = -ll_ref[...]
    softplus = jnp.maximum(z, 0.0) + jnp.log1p(jnp.exp(-jnp.abs(z)))

    n_parts = tm // part
    gelu_gate = []
    for p in range(n_parts):
        xn = xn_scr[p * part:(p + 1) * part, :]
        gelu_gate.append(jax.nn.gelu(jnp.dot(xn, wg_ref[...], preferred_element_type=F32)))
        rec = jnp.dot(xn, wr_ref[...], preferred_element_type=F32)
        if nseg == 1:
            rbuf[0, SUBLANES + p * part:SUBLANES + (p + 1) * part, :] = rec
        else:
            for s in range(nseg):
                rbuf[s, SUBLANES:SUBLANES + tseg, :] = rec[s * tseg:(s + 1) * tseg]

    h_run = None
    for p in range(n_parts):
        rows = slice(p * part, (p + 1) * part)

        segs = [(0, p * part, part)] if nseg == 1 else [(s, 0, tseg) for s in range(nseg)]
        for s, start, length in segs:
            xc = cb_ref[...] + rbuf[s, pad + start:pad + start + length, :] * cw_ref[0:1, :]
            for k in range(1, conv_w):
                lo = pad + start + k
                xc = xc + rbuf[s, lo:lo + length, :] * cw_ref[k:k + 1, :]
            lo = start if nseg == 1 else s * tseg
            xc_scr[lo:lo + length, :] = xc

        for n in range(tc // blk):
            cols = slice(n * blk, (n + 1) * blk)
            xc_n = xc_scr[rows, cols]
            gate = jnp.dot(xc_n.astype(BF16), gw_ref[n], preferred_element_type=F32) + gb_ref[n]
            gate = jax.nn.sigmoid(gate)
            log_a = (-RG_C * gate[:, :blk]) * softplus[:, cols]
            a_n = jnp.exp(log_a)
            a_scr[rows, cols] = a_n
            b_scr[rows, cols] = jnp.sqrt(1.0 - a_n * a_n) * gate[:, blk:] * xc_n

        a = a_scr[rows, :].reshape(part // SUBLANES, SUBLANES, tc)
        b = b_scr[rows, :].reshape(part // SUBLANES, SUBLANES, tc)
        row = lax.broadcasted_iota(jnp.int32, a.shape, 1)
        shift = 1
        while shift < SUBLANES:
            keep = row >= shift
            a_prev = pltpu.roll(a, shift, axis=1)
            b_prev = pltpu.roll(b, shift, axis=1)
            b = jnp.where(keep, a * b_prev + b, b)
            a = jnp.where(keep, a * a_prev, a)
            shift *= 2
        a_scr[rows, :] = a.reshape(part, tc)
        b_scr[rows, :] = b.reshape(part, tc)

        for s, start, length in segs:
            lo = start if nseg == 1 else s * tseg
            if nseg > 1 or p == 0:
                h_run = h0_ref[s] if tiles_per_seq == 1 else hcar[c]

            def group(i, h_prev, lo=lo):
                off = pl.multiple_of(lo + i * SUBLANES, SUBLANES)
                h = a_scr[pl.ds(off, SUBLANES), :] * h_prev + b_scr[pl.ds(off, SUBLANES), :]
                h_scr[pl.ds(off, SUBLANES), :] = h
                return h[SUBLANES - 1:SUBLANES, :]

            h_run = lax.fori_loop(0, length // SUBLANES, group, h_run, unroll=8)
            if nseg > 1 or p == n_parts - 1:
                hl_ref[s] = h_run
                if tiles_per_seq > 1:
                    hcar[c] = h_run

        out_ref[rows, :] += jnp.dot((h_scr[rows, :] * gelu_gate[p]).astype(BF16), wo_ref[...],
                                    preferred_element_type=F32)

    for s in range(nseg):
        new_hist = rbuf[s, pad + tseg:SUBLANES + tseg, :]
        nh_ref[s] = new_hist
        if tiles_per_seq > 1:
            ccar[c] = new_hist


def _rg_layer(x, hist, h0, t, norm_g, w_in, conv_w, conv_b, gate_w, gate_b, log_lam, w_out):
    m, d = x.shape
    n_seq = m // t
    c_rnn = w_out.shape[0]
    conv_width = conv_w.shape[0]
    blk = c_rnn // RG_BLOCKS
    tc = min(RG_COL_TILE, c_rnn)
    n_c = c_rnn // tc
    assert n_c >= 2
    tseg, nseg, tps = _row_tiling(n_seq, t)
    tm = tseg * nseg
    kern = functools.partial(_rg_kernel, tseg=tseg, nseg=nseg, tiles_per_seq=tps, blk=blk,
                             part=_part_rows(tseg, nseg))
    seq_blk = lambda r, c: (r // tps, 0, c)
    out, new_hist, h_last = pl.pallas_call(
        kern,
        grid=(m // tm, n_c),
        in_specs=[
            pl.BlockSpec(memory_space=pl.ANY),
            pl.BlockSpec((nseg, conv_width - 1, tc), seq_blk),
            pl.BlockSpec((nseg, 1, tc), seq_blk),
            pl.BlockSpec((1, d), lambda r, c: (0, 0)),
            pl.BlockSpec((d, tc), lambda r, c: (0, c)),
            pl.BlockSpec((d, tc), lambda r, c: (0, n_c + c)),
            pl.BlockSpec((conv_width, tc), lambda r, c: (0, c)),
            pl.BlockSpec((1, tc), lambda r, c: (0, c)),
            pl.BlockSpec((tc // blk, blk, 2 * blk), lambda r, c: (c, 0, 0)),
            pl.BlockSpec((tc // blk, 1, 2 * blk), lambda r, c: (c, 0, 0)),
            pl.BlockSpec((1, tc), lambda r, c: (0, c)),
            pl.BlockSpec((tc, d), lambda r, c: (c, 0)),
        ],
        out_specs=[
            pl.BlockSpec((tm, d), lambda r, c: (r, 0)),
            pl.BlockSpec((nseg, conv_width - 1, tc), lambda r, c: (r, 0, c)),
            pl.BlockSpec((nseg, 1, tc), lambda r, c: (r, 0, c)),
        ],
        out_shape=[
            jax.ShapeDtypeStruct((m, d), F32),
            jax.ShapeDtypeStruct((n_seq * tps, conv_width - 1, c_rnn), F32),
            jax.ShapeDtypeStruct((n_seq * tps, 1, c_rnn), F32),
        ],
        scratch_shapes=[
            pltpu.VMEM((tm, d), F32),
            pltpu.SemaphoreType.DMA,
            pltpu.VMEM((tm, d), BF16),
            pltpu.VMEM((nseg, SUBLANES + tseg, tc), F32),
            pltpu.VMEM((tm, tc), F32),
            pltpu.VMEM((tm, tc), F32),
            pltpu.VMEM((tm, tc), F32),
            pltpu.VMEM((tm, tc), F32),
            pltpu.VMEM((n_c, conv_width - 1, tc), F32),
            pltpu.VMEM((n_c, 1, tc), F32),
        ],
        compiler_params=_params(2),
        name="rg_layer",
    )(x, hist, h0, norm_g.reshape(1, d), w_in, w_in, conv_w, conv_b.reshape(1, c_rnn),
      gate_w, gate_b.reshape(RG_BLOCKS, 1, 2 * blk), log_lam.reshape(1, c_rnn), w_out)
    return out, _last_tile(new_hist, n_seq, tps), _last_tile(h_last, n_seq, tps)


def _ffn_kernel(x_hbm, hist_ref, g_ref, wg_ref, wv_ref, cwg_ref, cwv_ref, cbg_ref, cbv_ref, wd_ref,
                fg_ref, out_ref, nh_ref,
                x_buf, x_sem, xn_scr, ubuf, hid_scr, ccar,
                *, tseg, nseg, tiles_per_seq, final_norm, part):
    r = pl.program_id(0)
    c = pl.program_id(1)
    n_c = pl.num_programs(1)
    conv_w = cwg_ref.shape[0]
    tm = tseg * nseg
    tb = r % tiles_per_seq
    pad = SUBLANES - (conv_w - 1)
    halves = ((wg_ref, cwg_ref, cbg_ref), (wv_ref, cwv_ref, cbv_ref))

    def consume():
        x = x_buf[...]
        xn_scr[...] = _rmsnorm_rows(x, g_ref[...]).astype(BF16)
        out_ref[...] = x

    _stream_x_tile(x_hbm, x_buf, x_sem, consume)

    for s in range(nseg):
        for half in range(2):
            def load_state(half=half, s=s):
                for k in range(conv_w - 1):
                    ubuf[half, s, pad + k:pad + k + 1, :] = hist_ref[s, k, half:half + 1, :]

            if tiles_per_seq == 1:
                load_state()
            else:
                pl.when(tb == 0)(load_state)

                @pl.when(tb != 0)
                def _():
                    ubuf[half, s, pad:SUBLANES, :] = ccar[c, half]

    n_parts = tm // part
    for p in range(n_parts):
        xn = xn_scr[p * part:(p + 1) * part, :]
        for half, (w_ref, _, _) in enumerate(halves):
            u = jnp.dot(xn, w_ref[...], preferred_element_type=F32)
            if nseg == 1:
                ubuf[half, 0, SUBLANES + p * part:SUBLANES + (p + 1) * part, :] = u
            else:
                for s in range(nseg):
                    ubuf[half, s, SUBLANES:SUBLANES + tseg, :] = u[s * tseg:(s + 1) * tseg]

    for p in range(n_parts):
        rows = slice(p * part, (p + 1) * part)
        segs = [(0, p * part, part)] if nseg == 1 else [(s, 0, tseg) for s in range(nseg)]
        for s, start, length in segs:
            conv = []
            for half, (_, cw_ref, cb_ref) in enumerate(halves):
                cv = cb_ref[...] + ubuf[half, s, pad + start:pad + start + length, :] * cw_ref[0:1, :]
                for k in range(1, conv_w):
                    lo = pad + start + k
                    cv = cv + ubuf[half, s, lo:lo + length, :] * cw_ref[k:k + 1, :]
                conv.append(cv)
            hid = (jax.nn.gelu(conv[0]) * conv[1]).astype(BF16)
            if nseg > 1:
                hid_scr[s * tseg:(s + 1) * tseg, :] = hid
        if nseg > 1:
            hid = hid_scr[...]
        out_ref[rows, :] += jnp.dot(hid, wd_ref[...], preferred_element_type=F32)

    for s in range(nseg):
        for half in range(2):
            for k in range(conv_w - 1):
                nh_ref[s, k, half:half + 1, :] = ubuf[half, s, pad + tseg + k:pad + tseg + k + 1, :]
            if tiles_per_seq > 1:
                ccar[c, half] = ubuf[half, s, pad + tseg:SUBLANES + tseg, :]

    if final_norm:
        @pl.when(c == n_c - 1)
        def _():
            out_ref[...] = _rmsnorm_rows(out_ref[...], fg_ref[...])


def _ffn_layer(x, hist, t, norm_g, w_up, conv_w, conv_b, w_down, final_g):
    m, d = x.shape
    n_seq = m // t
    d_ff = w_down.shape[0]
    conv_width = conv_w.shape[0]
    tf = FF_COL_TILE
    assert d_ff % tf == 0
    n_c = d_ff // tf
    assert n_c >= 2
    tseg, nseg, tps = _row_tiling(n_seq, t)
    tm = tseg * nseg
    final_norm = final_g is not None
    if final_g is None:
        final_g = jnp.ones((d,), F32)
    kern = functools.partial(_ffn_kernel, tseg=tseg, nseg=nseg, tiles_per_seq=tps,
                             final_norm=final_norm, part=_part_rows(tseg, nseg))
    hist4 = hist.reshape(n_seq, conv_width - 1, 2, d_ff)
    conv_b2 = conv_b.reshape(1, 2 * d_ff)
    out, new_hist = pl.pallas_call(
        kern,
        grid=(m // tm, n_c),
        in_specs=[
            pl.BlockSpec(memory_space=pl.ANY),
            pl.BlockSpec((nseg, conv_width - 1, 2, tf), lambda r, c: (r // tps, 0, 0, c)),
            pl.BlockSpec((1, d), lambda r, c: (0, 0)),
            pl.BlockSpec((d, tf), lambda r, c: (0, c)),
            pl.BlockSpec((d, tf), lambda r, c: (0, n_c + c)),
            pl.BlockSpec((conv_width, tf), lambda r, c: (0, c)),
            pl.BlockSpec((conv_width, tf), lambda r, c: (0, n_c + c)),
            pl.BlockSpec((1, tf), lambda r, c: (0, c)),
            pl.BlockSpec((1, tf), lambda r, c: (0, n_c + c)),
            pl.BlockSpec((tf, d), lambda r, c: (c, 0)),
            pl.BlockSpec((1, d), lambda r, c: (0, 0)),
        ],
        out_specs=[
            pl.BlockSpec((tm, d), lambda r, c: (r, 0)),
            pl.BlockSpec((nseg, conv_width - 1, 2, tf), lambda r, c: (r, 0, 0, c)),
        ],
        out_shape=[
            jax.ShapeDtypeStruct((m, d), F32),
            jax.ShapeDtypeStruct((n_seq * tps, conv_width - 1, 2, d_ff), F32),
        ],
        scratch_shapes=[
            pltpu.VMEM((tm, d), F32),
            pltpu.SemaphoreType.DMA,
            pltpu.VMEM((tm, d), BF16),
            pltpu.VMEM((2, nseg, SUBLANES + tseg, tf), F32),
            pltpu.VMEM((tm if nseg > 1 else SUBLANES, tf), BF16),
            pltpu.VMEM((n_c, 2, conv_width - 1, tf), F32),
        ],
        compiler_params=_params(2),
        name="conv_ffn",
    )(x, hist4, norm_g.reshape(1, d), w_up, w_up, conv_w, conv_w, conv_b2, conv_b2, w_down,
      final_g.reshape(1, d))
    return out, _last_tile(new_hist, n_seq, tps).reshape(n_seq, conv_width - 1, 2 * d_ff)


def _q_kernel(x_ref, g_ref, w_ref, q_ref, xn_ref, *, scale):
    xn = _rmsnorm_rows(x_ref[...], g_ref[...]).astype(BF16)
    xn_ref[...] = xn
    q_ref[...] = (jnp.dot(xn, w_ref[...], preferred_element_type=F32) * scale).astype(BF16)


def _q_proj(x, norm_g, w_qkv, scale):
    m, d = x.shape
    d_att = w_qkv.shape[1] // 3
    tm = min(QKV_ROW_TILE, m)
    return pl.pallas_call(
        functools.partial(_q_kernel, scale=scale),
        grid=(m // tm,),
        in_specs=[
            pl.BlockSpec((tm, d), lambda r: (r, 0)),
            pl.BlockSpec((1, d), lambda r: (0, 0)),
            pl.BlockSpec((d, d_att), lambda r: (0, 0), pipeline_mode=pl.Buffered(1)),
        ],
        out_specs=[
            pl.BlockSpec((tm, d_att), lambda r: (r, 0)),
            pl.BlockSpec((tm, d), lambda r: (r, 0)),
        ],
        out_shape=[
            jax.ShapeDtypeStruct((m, d_att), BF16),
            jax.ShapeDtypeStruct((m, d), BF16),
        ],
        compiler_params=_params(1),
        name="q_proj",
    )(x, norm_g.reshape(1, d), w_qkv)


def _kv_kernel(xn_ref, w_ref, heads_ref, flat_ref):
    y = jnp.dot(xn_ref[...], w_ref[...], preferred_element_type=F32)
    flat_ref[...] = y.astype(BF16)
    heads_ref[...] = pltpu.einshape("m(hd)->mhd", y, h=heads_ref.shape[1])


def _kv_proj(xn, w_qkv, which, n_split):
    m, d = xn.shape
    d_att = w_qkv.shape[1] // 3
    tm = min(QKV_ROW_TILE, m)
    return pl.pallas_call(
        _kv_kernel,
        grid=(m // tm,),
        in_specs=[
            pl.BlockSpec((tm, d), lambda r: (r, 0)),
            pl.BlockSpec((d, d_att), lambda r: (0, which), pipeline_mode=pl.Buffered(1)),
        ],
        out_specs=[
            pl.BlockSpec((tm, n_split, d_att // n_split), lambda r: (r, 0, 0)),
            pl.BlockSpec((tm, d_att), lambda r: (r, 0)),
        ],
        out_shape=[
            jax.ShapeDtypeStruct((m, n_split, d_att // n_split), F32),
            jax.ShapeDtypeStruct((m, d_att), BF16),
        ],
        compiler_params=_params(1),
        name="kv_proj",
    )(xn, w_qkv)


def _diff_lambda(lam_ref, lambda_init):
    lam = lam_ref[...]
    s1 = jnp.sum(lam[0:1] * lam[1:2], axis=-1, keepdims=True)
    s2 = jnp.sum(lam[2:3] * lam[3:4], axis=-1, keepdims=True)
    return jnp.exp(s1) - jnp.exp(s2) + lambda_init


def _head_norm(o, subln_ref, lambda_init):
    return _rmsnorm_rows(o, subln_ref[...]) * (1.0 - lambda_init)


def _nt_dot(a, b):
    return lax.dot_general(a, b, (((1,), (1,)), ((), ())), preferred_element_type=F32)


def _attn_prompt_kernel(q_ref, k_ref, v_ref, lam_ref, subln_ref, o_ref, m_scr, l_scr, acc_scr,
                        *, lambda_init):
    i = pl.program_id(2)
    tq = q_ref.shape[0]
    hw = q_ref.shape[1]
    hd = hw // 2
    q = q_ref[...]

    def fold(x, op):
        acc = x[:, 0:LANES]
        for g in range(1, x.shape[1] // LANES):
            acc = op(acc, x[:, g * LANES:(g + 1) * LANES])
        return acc

    def spread(x, width):
        return jnp.concatenate([x] * (width // LANES), axis=1)

    def row_max(s):
        return jnp.broadcast_to(jnp.max(fold(s, jnp.maximum), axis=1, keepdims=True), (tq, LANES))

    off = pl.multiple_of(i * tq, tq)
    k_blk = k_ref[pl.ds(off, tq), :]
    v_blk = v_ref[pl.ds(off, tq), :]
    q_chunk = lax.broadcasted_iota(jnp.int32, (tq, tq), 0) // CHUNK
    k_chunk = lax.broadcasted_iota(jnp.int32, (tq, tq), 1) // CHUNK
    visible = k_chunk <= q_chunk
    for j in range(2):
        cols = slice(j * hd, (j + 1) * hd)
        s = jnp.where(visible, _nt_dot(q[:, cols], k_blk[:, cols]), NEG_INF)
        m = row_max(s)
        p = jnp.exp(s - spread(m, tq))
        m_scr[j] = m
        l_scr[j] = fold(p, jnp.add)
        acc_scr[j] = jnp.dot(p.astype(BF16), v_blk, preferred_element_type=F32)

    def block(kb, carry):
        off = pl.multiple_of(kb * tq, tq)
        k_blk = k_ref[pl.ds(off, tq), :]
        v_blk = v_ref[pl.ds(off, tq), :]
        for j in range(2):
            cols = slice(j * hd, (j + 1) * hd)
            s = _nt_dot(q[:, cols], k_blk[:, cols])
            m_old = m_scr[j]
            m_new = jnp.maximum(m_old, row_max(s))
            alpha = jnp.exp(m_old - m_new)
            p = jnp.exp(s - spread(m_new, tq))
            m_scr[j] = m_new
            l_scr[j] = alpha * l_scr[j] + fold(p, jnp.add)
            acc_scr[j] = spread(alpha, hw) * acc_scr[j] + jnp.dot(p.astype(BF16), v_blk,
                                                                  preferred_element_type=F32)
        return carry

    lax.fori_loop(0, i, block, 0)

    lam = _diff_lambda(lam_ref, lambda_init)
    outs = [acc_scr[j] / jnp.sum(l_scr[j], axis=1, keepdims=True) for j in range(2)]
    o = outs[0] - lam * outs[1]
    o_ref[...] = _head_norm(o, subln_ref, lambda_init).astype(BF16)


def _attn_prompt(q, kb, vb, t, lam_p, subln, lambda_init):
    m, d_att = q.shape
    n_seq = m // t
    hw = d_att // N_HEADS
    tq = min(ATTN_TILE, t)
    assert t % tq == 0 and tq % CHUNK == 0
    n_q = t // tq
    return pl.pallas_call(
        functools.partial(_attn_prompt_kernel, lambda_init=lambda_init),
        grid=(n_seq, N_HEADS, n_q),
        in_specs=[
            pl.BlockSpec((tq, hw), lambda b, h, i: (b * n_q + i, h)),
            pl.BlockSpec((t, hw), lambda b, h, i: (b, h)),
            pl.BlockSpec((t, hw), lambda b, h, i: (b, h)),
            pl.BlockSpec(lam_p.shape, lambda b, h, i: (0, 0)),
            pl.BlockSpec((1, hw), lambda b, h, i: (0, 0)),
        ],
        out_specs=pl.BlockSpec((tq, hw), lambda b, h, i: (b * n_q + i, h)),
        out_shape=jax.ShapeDtypeStruct((m, d_att), BF16),
        scratch_shapes=[
            pltpu.VMEM((2, tq, LANES), F32),
            pltpu.VMEM((2, tq, LANES), F32),
            pltpu.VMEM((2, tq, hw), F32),
        ],
        compiler_params=_params(3),
        name="attn_prompt",
    )(q, kb, vb, lam_p, subln.reshape(1, hw))


def _attn_sample_kernel(q_ref, ck_ref, cv_ref, kn_ref, vn_ref, lam_ref, subln_ref, o_ref,
                        *, lambda_init):
    hd = q_ref.shape[1] // 2
    q = q_ref[...]
    k_cache = ck_ref[0].astype(BF16)
    v_cache = cv_ref[0].astype(BF16)
    k_new = kn_ref[...]
    v_new = vn_ref[...]
    outs = []
    for j in range(2):
        cols = slice(j * hd, (j + 1) * hd)
        s_c = _nt_dot(q[:, cols], k_cache[:, cols])
        s_n = _nt_dot(q[:, cols], k_new[:, cols])
        m = jnp.maximum(jnp.max(s_c, axis=-1, keepdims=True), jnp.max(s_n, axis=-1, keepdims=True))
        p_c = jnp.exp(s_c - m)
        p_n = jnp.exp(s_n - m)
        l = jnp.sum(p_c, axis=-1, keepdims=True) + jnp.sum(p_n, axis=-1, keepdims=True)
        pv = (jnp.dot(p_c.astype(BF16), v_cache, preferred_element_type=F32)
              + jnp.dot(p_n.astype(BF16), v_new, preferred_element_type=F32))
        outs.append(pv / l)
    lam = _diff_lambda(lam_ref, lambda_init)
    o = outs[0] - lam * outs[1]
    o_ref[...] = _head_norm(o, subln_ref, lambda_init).astype(BF16)


def _attn_sample(q, cache_k, cache_v, kb, vb, t, lam_p, subln, lambda_init):
    m, d_att = q.shape
    n_seq = m // t
    past = cache_k.shape[1]
    hw = d_att // N_HEADS
    new_blk = pl.BlockSpec((t, hw), lambda b, h: (b, h))
    cache_blk = pl.BlockSpec((1, past, hw), lambda b, h: (b, 0, h))
    return pl.pallas_call(
        functools.partial(_attn_sample_kernel, lambda_init=lambda_init),
        grid=(n_seq, N_HEADS),
        in_specs=[
            new_blk, cache_blk, cache_blk, new_blk, new_blk,
            pl.BlockSpec(lam_p.shape, lambda b, h: (0, 0)),
            pl.BlockSpec((1, hw), lambda b, h: (0, 0)),
        ],
        out_specs=new_blk,
        out_shape=jax.ShapeDtypeStruct((m, d_att), BF16),
        compiler_params=_params(2),
        name="attn_sample",
    )(q, cache_k, cache_v, kb, vb, lam_p, subln.reshape(1, hw))


def _proj_residual_kernel(o_ref, w_ref, x_ref, out_ref):
    out_ref[...] = x_ref[...] + jnp.dot(o_ref[...], w_ref[...], preferred_element_type=F32)


def _proj_residual(o, w, x):
    m, kdim = o.shape
    n = w.shape[1]
    tm = min(PROJ_ROW_TILE, m)
    tn = PROJ_COL_TILE
    return pl.pallas_call(
        _proj_residual_kernel,
        grid=(m // tm, n // tn),
        in_specs=[
            pl.BlockSpec((tm, kdim), lambda r, c: (r, 0)),
            pl.BlockSpec((kdim, tn), lambda r, c: (0, c)),
            pl.BlockSpec((tm, tn), lambda r, c: (r, c)),
        ],
        out_specs=pl.BlockSpec((tm, tn), lambda r, c: (r, c)),
        out_shape=jax.ShapeDtypeStruct((m, n), F32),
        compiler_params=_params(2),
        name="attn_out_proj",
    )(o, w, x)


def kernel(x_prompt, x_sample, state_rglru_conv, state_rglru_h, cache_attn_k, cache_attn_v, state_ffn_conv, rg_norm, rg_w_in, rg_conv_w, rg_conv_b, rg_gate_w, rg_gate_b, rg_log_lambda, rg_w_out, at_norm, at_w_qkv, at_lambda, at_subln, at_w_out, ffn_norm, ffn_w_up, ffn_conv_w, ffn_conv_b, ffn_w_down, final_norm):
    bp, tp, d = x_prompt.shape
    bs, ts, _ = x_sample.shape
    depth = ffn_norm.shape[0]
    c_rnn = rg_w_out.shape[1]
    d_att = at_w_out.shape[1]
    head_dim = d_att // (2 * N_HEADS)
    past = cache_attn_k.shape[2]

    xp = x_prompt.reshape(bp * tp, d)
    xs = x_sample.reshape(bs * ts, d)
    p_rg_conv, p_rg_h, p_k, p_v, p_ffn = [], [], [], [], []
    s_rg_conv, s_rg_h, s_k, s_v, s_ffn = [], [], [], [], []
    for layer in range(depth):
        j = layer // 2
        if layer % 2 == 0:
            w = (rg_norm[j], rg_w_in[j].astype(BF16), rg_conv_w[j], rg_conv_b[j],
                 rg_gate_w[j].astype(BF16), rg_gate_b[j], rg_log_lambda[j], rg_w_out[j].astype(BF16))
            zero_hist = jnp.zeros((bp, rg_conv_w.shape[1] - 1, c_rnn), F32)
            zero_h = jnp.zeros((bp, 1, c_rnn), F32)
            xp, cp, hp = _rg_layer(xp, zero_hist, zero_h, tp, *w)
            xs, cs, hs = _rg_layer(xs, state_rglru_conv[j], state_rglru_h[j].reshape(bs, 1, c_rnn),
                                   ts, *w)
            p_rg_conv.append(cp); p_rg_h.append(hp.reshape(bp, c_rnn))
            s_rg_conv.append(cs); s_rg_h.append(hs.reshape(bs, c_rnn))
        else:
            lambda_init = 0.8 - 0.6 * math.exp(-0.3 * layer)
            w_qkv = at_w_qkv[j].astype(BF16)
            w_out = at_w_out[j].astype(BF16)
            scale = head_dim ** -0.5
            qp, xnp = _q_proj(xp, at_norm[j], w_qkv, scale)
            kp, kpb = _kv_proj(xnp, w_qkv, 1, 2 * N_HEADS)
            vp, vpb = _kv_proj(xnp, w_qkv, 2, N_HEADS)
            op = _attn_prompt(qp, kpb, vpb, tp, at_lambda[j], at_subln[j], lambda_init)
            xp = _proj_residual(op, w_out, xp)
            qs, xns = _q_proj(xs, at_norm[j], w_qkv, scale)
            kn, knb = _kv_proj(xns, w_qkv, 1, 2 * N_HEADS)
            vn, vnb = _kv_proj(xns, w_qkv, 2, N_HEADS)
            os_ = _attn_sample(qs, cache_attn_k[j].reshape(bs, past, d_att),
                               cache_attn_v[j].reshape(bs, past, d_att), knb, vnb, ts,
                               at_lambda[j], at_subln[j], lambda_init)
            xs = _proj_residual(os_, w_out, xs)
            p_k.append(kp.reshape(bp, tp, 2 * N_HEADS, head_dim))
            p_v.append(vp.reshape(bp, tp, N_HEADS, 2 * head_dim))
            s_k.append(kn.reshape(bs, ts, 2 * N_HEADS, head_dim))
            s_v.append(vn.reshape(bs, ts, N_HEADS, 2 * head_dim))
        last = layer == depth - 1
        fw = (ffn_norm[layer], ffn_w_up[layer].astype(BF16), ffn_conv_w[layer], ffn_conv_b[layer],
              ffn_w_down[layer].astype(BF16), final_norm if last else None)
        zero_ffn = jnp.zeros((bp,) + state_ffn_conv.shape[2:], F32)
        xp, fcp = _ffn_layer(xp, zero_ffn, tp, *fw)
        xs, fcs = _ffn_layer(xs, state_ffn_conv[layer], ts, *fw)
        p_ffn.append(fcp); s_ffn.append(fcs)
    stack = lambda arrs: arrs[0][None] if len(arrs) == 1 else jnp.stack(arrs)
    return (xp.reshape(bp, tp, d), xs.reshape(bs, ts, d),
            stack(p_rg_conv), stack(p_rg_h), stack(p_k), stack(p_v), stack(p_ffn),
            stack(s_rg_conv), stack(s_rg_h), stack(s_k), stack(s_v), stack(s_ffn))
```

```python
import functools
import math

import jax
import jax.numpy as jnp
from jax import lax
from jax.experimental import pallas as pl
from jax.experimental.pallas import tpu as pltpu

F32 = jnp.float32
BF16 = jnp.bfloat16

EPS = 1e-6
NEG_INF = -1e30
CHUNK = 64
N_HEADS = 8
RG_BLOCKS = 16
RG_C = 8.0
SUBLANES = 8
LANES = 128
VMEM_LIMIT_BYTES = 58 * 1024 * 1024

LAYER_ROW_TILE = 1024
PART_ROWS = 512
PROJ_ROW_TILE = 1024
QKV_ROW_TILE = 512
RG_COL_TILE = 512
FF_COL_TILE = 512
PROJ_COL_TILE = 512
ATTN_TILE = 512
SAMPLE_KEY_TILE = 512


def _params(n_axes):
    return pltpu.CompilerParams(
        dimension_semantics=("arbitrary",) * n_axes,
        vmem_limit_bytes=VMEM_LIMIT_BYTES,
    )


def _rmsnorm_rows(x, g):
    ms = jnp.mean(x * x, axis=-1, keepdims=True)
    return (x * lax.rsqrt(ms + EPS)) * g


def _row_tiling(n_seq, t):
    if t >= LAYER_ROW_TILE:
        assert t % LAYER_ROW_TILE == 0
        return LAYER_ROW_TILE, 1, t // LAYER_ROW_TILE
    assert LAYER_ROW_TILE % t == 0 and t % SUBLANES == 0
    nseg = min(LAYER_ROW_TILE // t, n_seq)
    assert n_seq % nseg == 0
    return t, nseg, 1


def _part_rows(tseg, nseg):
    tm = tseg * nseg
    if nseg == 1 and tm % PART_ROWS == 0:
        return PART_ROWS
    return tm


def _last_tile(state, n_seq, tiles_per_seq):
    if tiles_per_seq == 1:
        return state
    return state.reshape((n_seq, tiles_per_seq) + state.shape[1:])[:, -1]


def _x_tile_copy(x_hbm, x_buf, sem, r):
    tm = x_buf.shape[0]
    return pltpu.make_async_copy(x_hbm.at[pl.ds(pl.multiple_of(r * tm, tm), tm)], x_buf, sem)


def _stream_x_tile(x_hbm, x_buf, sem, consume):
    r = pl.program_id(0)
    c = pl.program_id(1)
    n_r = pl.num_programs(0)

    @pl.when(c == 0)
    def _():
        @pl.when(r == 0)
        def _():
            _x_tile_copy(x_hbm, x_buf, sem, r).start()

        _x_tile_copy(x_hbm, x_buf, sem, r).wait()
        consume()

    @pl.when(jnp.logical_and(c == 1, r + 1 < n_r))
    def _():
        _x_tile_copy(x_hbm, x_buf, sem, r + 1).start()


def _rg_kernel(x_hbm, hist_ref, h0_ref, g_ref, wg_ref, wr_ref, cw_ref, cb_ref, gw_ref, gb_ref,
               ll_ref, wo_ref,
               out_ref, nh_ref, hl_ref,
               x_buf, x_sem, xn_scr, rbuf, xc_scr, a_scr, b_scr, ccar, hcar,
               *, tseg, nseg, tiles_per_seq, blk, part):
    r = pl.program_id(0)
    c = pl.program_id(1)
    conv_w = cw_ref.shape[0]
    tc = wg_ref.shape[1]
    tm = tseg * nseg
    tb = r % tiles_per_seq
    pad = SUBLANES - (conv_w - 1)

    def consume():
        x = x_buf[...]
        xn_scr[...] = _rmsnorm_rows(x, g_ref[...]).astype(BF16)
        out_ref[...] = x

    _stream_x_tile(x_hbm, x_buf, x_sem, consume)

    for s in range(nseg):
        if tiles_per_seq == 1:
            rbuf[s, pad:SUBLANES, :] = hist_ref[s]
        else:
            @pl.when(tb == 0)
            def _():
                rbuf[s, pad:SUBLANES, :] = hist_ref[s]
                hcar[c] = h0_ref[s]

            @pl.when(tb != 0)
            def _():
                rbuf[s, pad:SUBLANES, :] = ccar[c]

    z = -ll_ref[...]
    softplus = jnp.maximum(z, 0.0) + jnp.log1p(jnp.exp(-jnp.abs(z)))

    n_parts = tm // part
    gelu_gate = []
    for p in range(n_parts):
        xn = xn_scr[p * part:(p + 1) * part, :]
        gelu_gate.append(jax.nn.gelu(jnp.dot(xn, wg_ref[...], preferred_element_type=F32)))
        rec = jnp.dot(xn, wr_ref[...], preferred_element_type=F32)
        if nseg == 1:
            rbuf[0, SUBLANES + p * part:SUBLANES + (p + 1) * part, :] = rec
        else:
            for s in range(nseg):
                rbuf[s, SUBLANES:SUBLANES + tseg, :] = rec[s * tseg:(s + 1) * tseg]

    h_run = None
    for p in range(n_parts):
        rows = slice(p * part, (p + 1) * part)

        segs = [(0, p * part, part)] if nseg == 1 else [(s, 0, tseg) for s in range(nseg)]
        for s, start, length in segs:
            xc = cb_ref[...] + rbuf[s, pad + start:pad + start + length, :] * cw_ref[0:1, :]
            for k in range(1, conv_w):
                lo = pad + start + k
                xc = xc + rbuf[s, lo:lo + length, :] * cw_ref[k:k + 1, :]
            lo = start if nseg == 1 else s * tseg
            xc_scr[lo:lo + length, :] = xc

        for n in range(tc // blk):
            cols = slice(n * blk, (n + 1) * blk)
            xc_n = xc_scr[rows, cols]
            gate = jnp.dot(xc_n.astype(BF16), gw_ref[n], preferred_element_type=F32) + gb_ref[n]
            gate = jax.nn.sigmoid(gate)
            log_a = (-RG_C * gate[:, :blk]) * softplus[:, cols]
            a_n = jnp.exp(log_a)
            a_scr[rows, cols] = a_n
            var = 1.0 - a_n * a_n
            mult = jnp.where(var > 0.0, var * lax.rsqrt(var), 0.0)
            b_scr[rows, cols] = mult * gate[:, blk:] * xc_n

        a = a_scr[rows, :].reshape(part // SUBLANES, SUBLANES, tc)
        b = b_scr[rows, :].reshape(part // SUBLANES, SUBLANES, tc)
        row = lax.broadcasted_iota(jnp.int32, a.shape, 1)
        shift = 1
        while shift < SUBLANES:
            keep = row >= shift
            a_prev = pltpu.roll(a, shift, axis=1)
            b_prev = pltpu.roll(b, shift, axis=1)
            b = jnp.where(keep, a * b_prev + b, b)
            a = jnp.where(keep, a * a_prev, a)
            shift *= 2
        a_scr[rows, :] = a.reshape(part, tc)
        b_scr[rows, :] = b.reshape(part, tc)

        for s, start, length in segs:
            lo = start if nseg == 1 else s * tseg
            if nseg > 1 or p == 0:
                h_run = h0_ref[s] if tiles_per_seq == 1 else hcar[c]

            def group(i, h_prev, lo=lo):
                off = pl.multiple_of(lo + i * SUBLANES, SUBLANES)
                h = a_scr[pl.ds(off, SUBLANES), :] * h_prev + b_scr[pl.ds(off, SUBLANES), :]
                b_scr[pl.ds(off, SUBLANES), :] = h
                return h[SUBLANES - 1:SUBLANES, :]

            h_run = lax.fori_loop(0, length // SUBLANES, group, h_run, unroll=8)
            if nseg > 1 or p == n_parts - 1:
                hl_ref[s] = h_run
                if tiles_per_seq > 1:
                    hcar[c] = h_run

        out_ref[rows, :] += jnp.dot((b_scr[rows, :] * gelu_gate[p]).astype(BF16), wo_ref[...],
                                    preferred_element_type=F32)

    for s in range(nseg):
        new_hist = rbuf[s, pad + tseg:SUBLANES + tseg, :]
        nh_ref[s] = new_hist
        if tiles_per_seq > 1:
            ccar[c] = new_hist


def _rg_layer(x, hist, h0, t, norm_g, w_in, conv_w, conv_b, gate_w, gate_b, log_lam, w_out):
    m, d = x.shape
    n_seq = m // t
    c_rnn = w_out.shape[0]
    conv_width = conv_w.shape[0]
    blk = c_rnn // RG_BLOCKS
    tc = min(RG_COL_TILE, c_rnn)
    n_c = c_rnn // tc
    assert n_c >= 2
    tseg, nseg, tps = _row_tiling(n_seq, t)
    tm = tseg * nseg
    kern = functools.partial(_rg_kernel, tseg=tseg, nseg=nseg, tiles_per_seq=tps, blk=blk,
                             part=_part_rows(tseg, nseg))
    seq_blk = lambda r, c: (r // tps, 0, c)
    out, new_hist, h_last = pl.pallas_call(
        kern,
        grid=(m // tm, n_c),
        in_specs=[
            pl.BlockSpec(memory_space=pl.ANY),
            pl.BlockSpec((nseg, conv_width - 1, tc), seq_blk),
            pl.BlockSpec((nseg, 1, tc), seq_blk),
            pl.BlockSpec((1, d), lambda r, c: (0, 0)),
            pl.BlockSpec((d, tc), lambda r, c: (0, c)),
            pl.BlockSpec((d, tc), lambda r, c: (0, n_c + c)),
            pl.BlockSpec((conv_width, tc), lambda r, c: (0, c)),
            pl.BlockSpec((1, tc), lambda r, c: (0, c)),
            pl.BlockSpec((tc // blk, blk, 2 * blk), lambda r, c: (c, 0, 0)),
            pl.BlockSpec((tc // blk, 1, 2 * blk), lambda r, c: (c, 0, 0)),
            pl.BlockSpec((1, tc), lambda r, c: (0, c)),
            pl.BlockSpec((tc, d), lambda r, c: (c, 0)),
        ],
        out_specs=[
            pl.BlockSpec((tm, d), lambda r, c: (r, 0)),
            pl.BlockSpec((nseg, conv_width - 1, tc), lambda r, c: (r, 0, c)),
            pl.BlockSpec((nseg, 1, tc), lambda r, c: (r, 0, c)),
        ],
        out_shape=[
            jax.ShapeDtypeStruct((m, d), F32),
            jax.ShapeDtypeStruct((n_seq * tps, conv_width - 1, c_rnn), F32),
            jax.ShapeDtypeStruct((n_seq * tps, 1, c_rnn), F32),
        ],
        scratch_shapes=[
            pltpu.VMEM((tm, d), F32),
            pltpu.SemaphoreType.DMA,
            pltpu.VMEM((tm, d), BF16),
            pltpu.VMEM((nseg, SUBLANES + tseg, tc), F32),
            pltpu.VMEM((tm, tc), F32),
            pltpu.VMEM((tm, tc), F32),
            pltpu.VMEM((tm, tc), F32),
            pltpu.VMEM((n_c, conv_width - 1, tc), F32),
            pltpu.VMEM((n_c, 1, tc), F32),
        ],
        compiler_params=_params(2),
        name="rg_layer",
    )(x, hist, h0, norm_g.reshape(1, d), w_in, w_in, conv_w, conv_b.reshape(1, c_rnn),
      gate_w, gate_b.reshape(RG_BLOCKS, 1, 2 * blk), log_lam.reshape(1, c_rnn), w_out)
    return out, _last_tile(new_hist, n_seq, tps), _last_tile(h_last, n_seq, tps)


def _ffn_kernel(x_hbm, hist_ref, g_ref, wg_ref, wv_ref, cwg_ref, cwv_ref, cbg_ref, cbv_ref, wd_ref,
                fg_ref, out_ref, nh_ref,
                x_buf, x_sem, xn_scr, ubuf, hid_scr, ccar,
                *, tseg, nseg, tiles_per_seq, final_norm, part):
    r = pl.program_id(0)
    c = pl.program_id(1)
    n_c = pl.num_programs(1)
    conv_w = cwg_ref.shape[0]
    tm = tseg * nseg
    tb = r % tiles_per_seq
    pad = SUBLANES - (conv_w - 1)
    halves = ((wg_ref, cwg_ref, cbg_ref), (wv_ref, cwv_ref, cbv_ref))

    def consume():
        x = x_buf[...]
        xn_scr[...] = _rmsnorm_rows(x, g_ref[...]).astype(BF16)
        out_ref[...] = x

    _stream_x_tile(x_hbm, x_buf, x_sem, consume)

    for s in range(nseg):
        for half in range(2):
            def load_state(half=half, s=s):
                for k in range(conv_w - 1):
                    ubuf[half, s, pad + k:pad + k + 1, :] = hist_ref[s, k, half:half + 1, :]

            if tiles_per_seq == 1:
                load_state()
            else:
                pl.when(tb == 0)(load_state)

                @pl.when(tb != 0)
                def _():
                    ubuf[half, s, pad:SUBLANES, :] = ccar[c, half]

    n_parts = tm // part
    for p in range(n_parts):
        xn = xn_scr[p * part:(p + 1) * part, :]
        for half, (w_ref, _, _) in enumerate(halves):
            u = jnp.dot(xn, w_ref[...], preferred_element_type=F32)
            if nseg == 1:
                ubuf[half, 0, SUBLANES + p * part:SUBLANES + (p + 1) * part, :] = u
            else:
                for s in range(nseg):
                    ubuf[half, s, SUBLANES:SUBLANES + tseg, :] = u[s * tseg:(s + 1) * tseg]

    for p in range(n_parts):
        rows = slice(p * part, (p + 1) * part)
        segs = [(0, p * part, part)] if nseg == 1 else [(s, 0, tseg) for s in range(nseg)]
        for s, start, length in segs:
            conv = []
            for half, (_, cw_ref, cb_ref) in enumerate(halves):
                cv = cb_ref[...] + ubuf[half, s, pad + start:pad + start + length, :] * cw_ref[0:1, :]
                for k in range(1, conv_w):
                    lo = pad + start + k
                    cv = cv + ubuf[half, s, lo:lo + length, :] * cw_ref[k:k + 1, :]
                conv.append(cv)
            hid = (jax.nn.gelu(conv[0]) * conv[1]).astype(BF16)
            if nseg > 1:
                hid_scr[s * tseg:(s + 1) * tseg, :] = hid
        if nseg > 1:
            hid = hid_scr[...]
        out_ref[rows, :] += jnp.dot(hid, wd_ref[...], preferred_element_type=F32)

    for s in range(nseg):
        for half in range(2):
            for k in range(conv_w - 1):
                nh_ref[s, k, half:half + 1, :] = ubuf[half, s, pad + tseg + k:pad + tseg + k + 1, :]
            if tiles_per_seq > 1:
                ccar[c, half] = ubuf[half, s, pad + tseg:SUBLANES + tseg, :]

    if final_norm:
        @pl.when(c == n_c - 1)
        def _():
            out_ref[...] = _rmsnorm_rows(out_ref[...], fg_ref[...])


def _ffn_layer(x, hist, t, norm_g, w_up, conv_w, conv_b, w_down, final_g):
    m, d = x.shape
    n_seq = m // t
    d_ff = w_down.shape[0]
    conv_width = conv_w.shape[0]
    tf = FF_COL_TILE
    assert d_ff % tf == 0
    n_c = d_ff // tf
    assert n_c >= 2
    tseg, nseg, tps = _row_tiling(n_seq, t)
    tm = tseg * nseg
    final_norm = final_g is not None
    if final_g is None:
        final_g = jnp.ones((d,), F32)
    kern = functools.partial(_ffn_kernel, tseg=tseg, nseg=nseg, tiles_per_seq=tps,
                             final_norm=final_norm, part=_part_rows(tseg, nseg))
    hist4 = hist.reshape(n_seq, conv_width - 1, 2, d_ff)
    conv_b2 = conv_b.reshape(1, 2 * d_ff)
    out, new_hist = pl.pallas_call(
        kern,
        grid=(m // tm, n_c),
        in_specs=[
            pl.BlockSpec(memory_space=pl.ANY),
            pl.BlockSpec((nseg, conv_width - 1, 2, tf), lambda r, c: (r // tps, 0, 0, c)),
            pl.BlockSpec((1, d), lambda r, c: (0, 0)),
            pl.BlockSpec((d, tf), lambda r, c: (0, c)),
            pl.BlockSpec((d, tf), lambda r, c: (0, n_c + c)),
            pl.BlockSpec((conv_width, tf), lambda r, c: (0, c)),
            pl.BlockSpec((conv_width, tf), lambda r, c: (0, n_c + c)),
            pl.BlockSpec((1, tf), lambda r, c: (0, c)),
            pl.BlockSpec((1, tf), lambda r, c: (0, n_c + c)),
            pl.BlockSpec((tf, d), lambda r, c: (c, 0)),
            pl.BlockSpec((1, d), lambda r, c: (0, 0)),
        ],
        out_specs=[
            pl.BlockSpec((tm, d), lambda r, c: (r, 0)),
            pl.BlockSpec((nseg, conv_width - 1, 2, tf), lambda r, c: (r, 0, 0, c)),
        ],
        out_shape=[
            jax.ShapeDtypeStruct((m, d), F32),
            jax.ShapeDtypeStruct((n_seq * tps, conv_width - 1, 2, d_ff), F32),
        ],
        scratch_shapes=[
            pltpu.VMEM((tm, d), F32),
            pltpu.SemaphoreType.DMA,
            pltpu.VMEM((tm, d), BF16),
            pltpu.VMEM((2, nseg, SUBLANES + tseg, tf), F32),
            pltpu.VMEM((tm if nseg > 1 else SUBLANES, tf), BF16),
            pltpu.VMEM((n_c, 2, conv_width - 1, tf), F32),
        ],
        compiler_params=_params(2),
        name="conv_ffn",
    )(x, hist4, norm_g.reshape(1, d), w_up, w_up, conv_w, conv_w, conv_b2, conv_b2, w_down,
      final_g.reshape(1, d))
    return out, _last_tile(new_hist, n_seq, tps).reshape(n_seq, conv_width - 1, 2 * d_ff)


def _q_kernel(x_ref, g_ref, w_ref, q_ref, xn_ref, *, scale):
    xn = _rmsnorm_rows(x_ref[...], g_ref[...]).astype(BF16)
    xn_ref[...] = xn
    q_ref[...] = (jnp.dot(xn, w_ref[...], preferred_element_type=F32) * scale).astype(BF16)


def _q_proj(x, norm_g, w_qkv, scale):
    m, d = x.shape
    d_att = w_qkv.shape[1] // 3
    tm = min(QKV_ROW_TILE, m)
    return pl.pallas_call(
        functools.partial(_q_kernel, scale=scale),
        grid=(m // tm,),
        in_specs=[
            pl.BlockSpec((tm, d), lambda r: (r, 0)),
            pl.BlockSpec((1, d), lambda r: (0, 0)),
            pl.BlockSpec((d, d_att), lambda r: (0, 0), pipeline_mode=pl.Buffered(1)),
        ],
        out_specs=[
            pl.BlockSpec((tm, d_att), lambda r: (r, 0)),
            pl.BlockSpec((tm, d), lambda r: (r, 0)),
        ],
        out_shape=[
            jax.ShapeDtypeStruct((m, d_att), BF16),
            jax.ShapeDtypeStruct((m, d), BF16),
        ],
        compiler_params=_params(1),
        name="q_proj",
    )(x, norm_g.reshape(1, d), w_qkv)


def _kv_kernel(xn_ref, w_ref, heads_ref, flat_ref):
    y = jnp.dot(xn_ref[...], w_ref[...], preferred_element_type=F32)
    flat_ref[...] = y.astype(BF16)
    heads_ref[...] = pltpu.einshape("m(hd)->mhd", y, h=heads_ref.shape[1])


def _kv_proj(xn, w_qkv, which, n_split):
    m, d = xn.shape
    d_att = w_qkv.shape[1] // 3
    tm = min(QKV_ROW_TILE, m)
    return pl.pallas_call(
        _kv_kernel,
        grid=(m // tm,),
        in_specs=[
            pl.BlockSpec((tm, d), lambda r: (r, 0)),
            pl.BlockSpec((d, d_att), lambda r: (0, which), pipeline_mode=pl.Buffered(1)),
        ],
        out_specs=[
            pl.BlockSpec((tm, n_split, d_att // n_split), lambda r: (r, 0, 0)),
            pl.BlockSpec((tm, d_att), lambda r: (r, 0)),
        ],
        out_shape=[
            jax.ShapeDtypeStruct((m, n_split, d_att // n_split), F32),
            jax.ShapeDtypeStruct((m, d_att), BF16),
        ],
        compiler_params=_params(1),
        name="kv_proj",
    )(xn, w_qkv)


def _diff_lambda(lam_ref, lambda_init):
    lam = lam_ref[...]
    s1 = jnp.sum(lam[0:1] * lam[1:2], axis=-1, keepdims=True)
    s2 = jnp.sum(lam[2:3] * lam[3:4], axis=-1, keepdims=True)
    return jnp.exp(s1) - jnp.exp(s2) + lambda_init


def _head_norm(o, subln_ref, lambda_init):
    return _rmsnorm_rows(o, subln_ref[...]) * (1.0 - lambda_init)


def _nt_dot(a, b):
    return lax.dot_general(a, b, (((1,), (1,)), ((), ())), preferred_element_type=F32)


def _attn_prompt_kernel(q_ref, k_ref, v_ref, lam_ref, subln_ref, o_ref, m_scr, l_scr, acc_scr,
                        *, lambda_init):
    i = pl.program_id(2)
    tq = q_ref.shape[0]
    hw = q_ref.shape[1]
    hd = hw // 2
    q = q_ref[...]

    def fold(x, op):
        acc = x[:, 0:LANES]
        for g in range(1, x.shape[1] // LANES):
            acc = op(acc, x[:, g * LANES:(g + 1) * LANES])
        return acc

    def spread(x, width):
        return jnp.concatenate([x] * (width // LANES), axis=1)

    def row_max(s):
        return jnp.broadcast_to(jnp.max(fold(s, jnp.maximum), axis=1, keepdims=True), (tq, LANES))

    off = pl.multiple_of(i * tq, tq)
    k_blk = k_ref[pl.ds(off, tq), :]
    v_blk = v_ref[pl.ds(off, tq), :]
    q_chunk = lax.broadcasted_iota(jnp.int32, (tq, tq), 0) // CHUNK
    k_chunk = lax.broadcasted_iota(jnp.int32, (tq, tq), 1) // CHUNK
    visible = k_chunk <= q_chunk
    for j in range(2):
        cols = slice(j * hd, (j + 1) * hd)
        s = jnp.where(visible, _nt_dot(q[:, cols], k_blk[:, cols]), NEG_INF)
        m = row_max(s)
        p = jnp.exp(s - spread(m, tq))
        m_scr[j] = m
        l_scr[j] = fold(p, jnp.add)
        acc_scr[j] = jnp.dot(p.astype(BF16), v_blk, preferred_element_type=F32)

    def block(kb, carry):
        off = pl.multiple_of(kb * tq, tq)
        k_blk = k_ref[pl.ds(off, tq), :]
        v_blk = v_ref[pl.ds(off, tq), :]
        for j in range(2):
            cols = slice(j * hd, (j + 1) * hd)
            s = _nt_dot(q[:, cols], k_blk[:, cols])
            m_old = m_scr[j]
            m_new = jnp.maximum(m_old, row_max(s))
            alpha = jnp.exp(m_old - m_new)
            p = jnp.exp(s - spread(m_new, tq))
            m_scr[j] = m_new
            l_scr[j] = alpha * l_scr[j] + fold(p, jnp.add)
            acc_scr[j] = spread(alpha, hw) * acc_scr[j] + jnp.dot(p.astype(BF16), v_blk,
                                                                  preferred_element_type=F32)
        return carry

    lax.fori_loop(0, i, block, 0)

    lam = _diff_lambda(lam_ref, lambda_init)
    outs = [acc_scr[j] / jnp.sum(l_scr[j], axis=1, keepdims=True) for j in range(2)]
    o = outs[0] - lam * outs[1]
    o_ref[...] = _head_norm(o, subln_ref, lambda_init).astype(BF16)


def _attn_prompt(q, kb, vb, t, lam_p, subln, lambda_init):
    m, d_att = q.shape
    n_seq = m // t
    hw = d_att // N_HEADS
    tq = min(ATTN_TILE, t)
    assert t % tq == 0 and tq % CHUNK == 0
    n_q = t // tq
    return pl.pallas_call(
        functools.partial(_attn_prompt_kernel, lambda_init=lambda_init),
        grid=(n_seq, N_HEADS, n_q),
        in_specs=[
            pl.BlockSpec((tq, hw), lambda b, h, i: (b * n_q + i, h)),
            pl.BlockSpec((t, hw), lambda b, h, i: (b, h)),
            pl.BlockSpec((t, hw), lambda b, h, i: (b, h)),
            pl.BlockSpec(lam_p.shape, lambda b, h, i: (0, 0)),
            pl.BlockSpec((1, hw), lambda b, h, i: (0, 0)),
        ],
        out_specs=pl.BlockSpec((tq, hw), lambda b, h, i: (b * n_q + i, h)),
        out_shape=jax.ShapeDtypeStruct((m, d_att), BF16),
        scratch_shapes=[
            pltpu.VMEM((2, tq, LANES), F32),
            pltpu.VMEM((2, tq, LANES), F32),
            pltpu.VMEM((2, tq, hw), F32),
        ],
        compiler_params=_params(3),
        name="attn_prompt",
    )(q, kb, vb, lam_p, subln.reshape(1, hw))


def _attn_sample_kernel(q_ref, ck_ref, cv_ref, kn_ref, vn_ref, lam_ref, subln_ref, o_ref,
                        m_scr, l_scr, acc_scr, *, lambda_init):
    c = pl.program_id(1)
    n_c = pl.num_programs(1)
    t, d_att = q_ref.shape
    hw = d_att // N_HEADS
    hd = hw // 2
    q = q_ref[...]

    @pl.when(c == 0)
    def _():
        m_scr[...] = jnp.full(m_scr.shape, NEG_INF, F32)
        l_scr[...] = jnp.zeros_like(l_scr)
        acc_scr[...] = jnp.zeros_like(acc_scr)

    def attend(k_all, v_all):
        n = k_all.shape[0]
        for h in range(N_HEADS):
            v_h = v_all[:, h * hw:(h + 1) * hw]
            for j in range(2):
                hm = 2 * h + j
                cols = slice(h * hw + j * hd, h * hw + (j + 1) * hd)
                s = _nt_dot(q[:, cols], k_all[:, cols])
                m_old = m_scr[hm]
                m_new = jnp.maximum(m_old, jnp.max(s, axis=1, keepdims=True))
                alpha = jnp.exp(m_old - m_new)
                p = jnp.exp(s - m_new[:, 0:1])
                if n % LANES == 0:
                    psum = p[:, 0:LANES]
                    for g in range(1, n // LANES):
                        psum = psum + p[:, g * LANES:(g + 1) * LANES]
                else:
                    psum = jnp.sum(p, axis=1, keepdims=True) * (1.0 / LANES)
                m_scr[hm] = m_new
                l_scr[hm] = alpha * l_scr[hm] + psum
                acc_scr[hm] = (jnp.concatenate([alpha] * (hw // LANES), axis=1) * acc_scr[hm]
                               + jnp.dot(p.astype(BF16), v_h, preferred_element_type=F32))

    attend(pltpu.einshape("mhd->m(hd)", ck_ref[0, 0]).astype(BF16),
           pltpu.einshape("mhd->m(hd)", cv_ref[0, 0]).astype(BF16))

    @pl.when(c == n_c - 1)
    def _():
        attend(kn_ref[...], vn_ref[...])
        lam = _diff_lambda(lam_ref, lambda_init)
        for h in range(N_HEADS):
            outs = [acc_scr[2 * h + j] / jnp.sum(l_scr[2 * h + j], axis=1, keepdims=True)
                    for j in range(2)]
            o = outs[0] - lam * outs[1]
            o_ref[:, h * hw:(h + 1) * hw] = _head_norm(o, subln_ref, lambda_init).astype(BF16)


def _attn_sample(q, cache_k, cache_v, layer, kb, vb, t, lam_p, subln, lambda_init):
    m, d_att = q.shape
    n_seq = m // t
    past = cache_k.shape[2]
    hw = d_att // N_HEADS
    pc = min(SAMPLE_KEY_TILE, past)
    assert past % pc == 0
    new_blk = pl.BlockSpec((t, d_att), lambda b, c: (b, 0))
    cache_blk = lambda arr: pl.BlockSpec((1, 1, pc) + arr.shape[3:], lambda b, c: (layer, b, c, 0, 0))
    return pl.pallas_call(
        functools.partial(_attn_sample_kernel, lambda_init=lambda_init),
        grid=(n_seq, past // pc),
        in_specs=[
            new_blk, cache_blk(cache_k), cache_blk(cache_v), new_blk, new_blk,
            pl.BlockSpec(lam_p.shape, lambda b, c: (0, 0)),
            pl.BlockSpec((1, hw), lambda b, c: (0, 0)),
        ],
        out_specs=new_blk,
        out_shape=jax.ShapeDtypeStruct((m, d_att), BF16),
        scratch_shapes=[
            pltpu.VMEM((2 * N_HEADS, t, LANES), F32),
            pltpu.VMEM((2 * N_HEADS, t, LANES), F32),
            pltpu.VMEM((2 * N_HEADS, t, hw), F32),
        ],
        compiler_params=_params(2),
        name="attn_sample",
    )(q, cache_k, cache_v, kb, vb, lam_p, subln.reshape(1, hw))


def _proj_residual_kernel(o_ref, w_ref, x_ref, out_ref):
    out_ref[...] = x_ref[...] + jnp.dot(o_ref[...], w_ref[...], preferred_element_type=F32)


def _proj_residual(o, w, x):
    m, kdim = o.shape
    n = w.shape[1]
    tm = min(PROJ_ROW_TILE, m)
    tn = PROJ_COL_TILE
    return pl.pallas_call(
        _proj_residual_kernel,
        grid=(m // tm, n // tn),
        in_specs=[
            pl.BlockSpec((tm, kdim), lambda r, c: (r, 0)),
            pl.BlockSpec((kdim, tn), lambda r, c: (0, c)),
            pl.BlockSpec((tm, tn), lambda r, c: (r, c)),
        ],
        out_specs=pl.BlockSpec((tm, tn), lambda r, c: (r, c)),
        out_shape=jax.ShapeDtypeStruct((m, n), F32),
        compiler_params=_params(2),
        name="attn_out_proj",
    )(o, w, x)


def kernel(x_prompt, x_sample, state_rglru_conv, state_rglru_h, cache_attn_k, cache_attn_v, state_ffn_conv, rg_norm, rg_w_in, rg_conv_w, rg_conv_b, rg_gate_w, rg_gate_b, rg_log_lambda, rg_w_out, at_norm, at_w_qkv, at_lambda, at_subln, at_w_out, ffn_norm, ffn_w_up, ffn_conv_w, ffn_conv_b, ffn_w_down, final_norm):
    bp, tp, d = x_prompt.shape
    bs, ts, _ = x_sample.shape
    depth = ffn_norm.shape[0]
    c_rnn = rg_w_out.shape[1]
    d_att = at_w_out.shape[1]
    head_dim = d_att // (2 * N_HEADS)

    xp = x_prompt.reshape(bp * tp, d)
    xs = x_sample.reshape(bs * ts, d)
    p_rg_conv, p_rg_h, p_k, p_v, p_ffn = [], [], [], [], []
    s_rg_conv, s_rg_h, s_k, s_v, s_ffn = [], [], [], [], []
    for layer in range(depth):
        j = layer // 2
        if layer % 2 == 0:
            w = (rg_norm[j], rg_w_in[j].astype(BF16), rg_conv_w[j], rg_conv_b[j],
                 rg_gate_w[j].astype(BF16), rg_gate_b[j], rg_log_lambda[j], rg_w_out[j].astype(BF16))
            zero_hist = jnp.zeros((bp, rg_conv_w.shape[1] - 1, c_rnn), F32)
            zero_h = jnp.zeros((bp, 1, c_rnn), F32)
            xp, cp, hp = _rg_layer(xp, zero_hist, zero_h, tp, *w)
            xs, cs, hs = _rg_layer(xs, state_rglru_conv[j], state_rglru_h[j].reshape(bs, 1, c_rnn),
                                   ts, *w)
            p_rg_conv.append(cp); p_rg_h.append(hp.reshape(bp, c_rnn))
            s_rg_conv.append(cs); s_rg_h.append(hs.reshape(bs, c_rnn))
        else:
            lambda_init = 0.8 - 0.6 * math.exp(-0.3 * layer)
            w_qkv = at_w_qkv[j].astype(BF16)
            w_out = at_w_out[j].astype(BF16)
            scale = head_dim ** -0.5
            qp, xnp = _q_proj(xp, at_norm[j], w_qkv, scale)
            kp, kpb = _kv_proj(xnp, w_qkv, 1, 2 * N_HEADS)
            vp, vpb = _kv_proj(xnp, w_qkv, 2, N_HEADS)
            op = _attn_prompt(qp, kpb, vpb, tp, at_lambda[j], at_subln[j], lambda_init)
            xp = _proj_residual(op, w_out, xp)
            qs, xns = _q_proj(xs, at_norm[j], w_qkv, scale)
            kn, knb = _kv_proj(xns, w_qkv, 1, 2 * N_HEADS)
            vn, vnb = _kv_proj(xns, w_qkv, 2, N_HEADS)
            os_ = _attn_sample(qs, cache_attn_k, cache_attn_v, j, knb, vnb, ts,
                               at_lambda[j], at_subln[j], lambda_init)
            xs = _proj_residual(os_, w_out, xs)
            p_k.append(kp.reshape(bp, tp, 2 * N_HEADS, head_dim))
            p_v.append(vp.reshape(bp, tp, N_HEADS, 2 * head_dim))
            s_k.append(kn.reshape(bs, ts, 2 * N_HEADS, head_dim))
            s_v.append(vn.reshape(bs, ts, N_HEADS, 2 * head_dim))
        last = layer == depth - 1
        fw = (ffn_norm[layer], ffn_w_up[layer].astype(BF16), ffn_conv_w[layer], ffn_conv_b[layer],
              ffn_w_down[layer].astype(BF16), final_norm if last else None)
        zero_ffn = jnp.zeros((bp,) + state_ffn_conv.shape[2:], F32)
        xp, fcp = _ffn_layer(xp, zero_ffn, tp, *fw)
        xs, fcs = _ffn_layer(xs, state_ffn_conv[layer], ts, *fw)
        p_ffn.append(fcp); s_ffn.append(fcs)
    stack = lambda arrs: arrs[0][None] if len(arrs) == 1 else jnp.stack(arrs)
    return (xp.reshape(bp, tp, d), xs.reshape(bs, ts, d),
            stack(p_rg_conv), stack(p_rg_h), stack(p_k), stack(p_v), stack(p_ffn),
            stack(s_rg_conv), stack(s_rg_h), stack(s_k), stack(s_v), stack(s_ffn))
```

```python
import functools
import math

import jax
import jax.numpy as jnp
from jax import lax
from jax.experimental import pallas as pl
from jax.experimental.pallas import tpu as pltpu

F32 = jnp.float32
BF16 = jnp.bfloat16

EPS = 1e-6
NEG_INF = -1e30
CHUNK = 64
N_HEADS = 8
RG_BLOCKS = 16
RG_C = 8.0
SUBLANES = 8
LANES = 128
VMEM_LIMIT_BYTES = 58 * 1024 * 1024

LAYER_ROW_TILE = 1024
PART_ROWS = 512
PROJ_ROW_TILE = 1024
QKV_ROW_TILE = 512
RG_COL_TILE = 512
FF_COL_TILE = 512
PROJ_COL_TILE = 512
ATTN_TILE = 512
SAMPLE_KEY_TILE = 512


def _params(n_axes):
    return pltpu.CompilerParams(
        dimension_semantics=("arbitrary",) * n_axes,
        vmem_limit_bytes=VMEM_LIMIT_BYTES,
    )


def _rmsnorm_rows(x, g):
    ms = jnp.mean(x * x, axis=-1, keepdims=True)
    return (x * lax.rsqrt(ms + EPS)) * g


def _row_tiling(n_seq, t):
    if t >= LAYER_ROW_TILE:
        assert t % LAYER_ROW_TILE == 0
        return LAYER_ROW_TILE, 1, t // LAYER_ROW_TILE
    assert LAYER_ROW_TILE % t == 0 and t % SUBLANES == 0
    nseg = min(LAYER_ROW_TILE // t, n_seq)
    assert n_seq % nseg == 0
    return t, nseg, 1


def _part_rows(tseg, nseg):
    tm = tseg * nseg
    if nseg == 1 and tm % PART_ROWS == 0:
        return PART_ROWS
    return tm


def _last_tile(state, n_seq, tiles_per_seq):
    if tiles_per_seq == 1:
        return state
    return state.reshape((n_seq, tiles_per_seq) + state.shape[1:])[:, -1]


def _x_tile_copy(x_hbm, x_buf, sem, r):
    tm = x_buf.shape[0]
    return pltpu.make_async_copy(x_hbm.at[pl.ds(pl.multiple_of(r * tm, tm), tm)], x_buf, sem)


def _stream_x_tile(x_hbm, x_buf, sem, consume):
    r = pl.program_id(0)
    c = pl.program_id(1)
    n_r = pl.num_programs(0)

    @pl.when(c == 0)
    def _():
        @pl.when(r == 0)
        def _():
            _x_tile_copy(x_hbm, x_buf, sem, r).start()

        _x_tile_copy(x_hbm, x_buf, sem, r).wait()
        consume()

    @pl.when(jnp.logical_and(c == 1, r + 1 < n_r))
    def _():
        _x_tile_copy(x_hbm, x_buf, sem, r + 1).start()


def _rg_kernel(x_hbm, hist_ref, h0_ref, g_ref, wg_ref, wr_ref, cw_ref, cb_ref, gw_ref, gb_ref,
               ll_ref, wo_ref,
               out_ref, nh_ref, hl_ref,
               x_buf, x_sem, xn_scr, rbuf, xc_scr, a_scr, b_scr, ccar, hcar,
               *, tseg, nseg, tiles_per_seq, blk, part):
    r = pl.program_id(0)
    c = pl.program_id(1)
    conv_w = cw_ref.shape[0]
    tc = wg_ref.shape[1]
    tm = tseg * nseg
    tb = r % tiles_per_seq
    pad = SUBLANES - (conv_w - 1)

    def consume():
        x = x_buf[...]
        xn_scr[...] = _rmsnorm_rows(x, g_ref[...]).astype(BF16)
        out_ref[...] = x

    _stream_x_tile(x_hbm, x_buf, x_sem, consume)

    for s in range(nseg):
        if tiles_per_seq == 1:
            rbuf[s, pad:SUBLANES, :] = hist_ref[s]
        else:
            @pl.when(tb == 0)
            def _():
                rbuf[s, pad:SUBLANES, :] = hist_ref[s]
                hcar[c] = h0_ref[s]

            @pl.when(tb != 0)
            def _():
                rbuf[s, pad:SUBLANES, :] = ccar[c]

    z = -ll_ref[...]
    softplus = jnp.maximum(z, 0.0) + jnp.log1p(jnp.exp(-jnp.abs(z)))

    n_parts = tm // part
    gelu_gate = []
    for p in range(n_parts):
        xn = xn_scr[p * part:(p + 1) * part, :]
        gelu_gate.append(jax.nn.gelu(jnp.dot(xn, wg_ref[...], preferred_element_type=F32)))
        rec = jnp.dot(xn, wr_ref[...], preferred_element_type=F32)
        if nseg == 1:
            rbuf[0, SUBLANES + p * part:SUBLANES + (p + 1) * part, :] = rec
        else:
            for s in range(nseg):
                rbuf[s, SUBLANES:SUBLANES + tseg, :] = rec[s * tseg:(s + 1) * tseg]

    h_run = None
    for p in range(n_parts):
        rows = slice(p * part, (p + 1) * part)

        segs = [(0, p * part, part)] if nseg == 1 else [(s, 0, tseg) for s in range(nseg)]
        for s, start, length in segs:
            xc = cb_ref[...] + rbuf[s, pad + start:pad + start + length, :] * cw_ref[0:1, :]
            for k in range(1, conv_w):
                lo = pad + start + k
                xc = xc + rbuf[s, lo:lo + length, :] * cw_ref[k:k + 1, :]
            lo = start if nseg == 1 else s * tseg
            xc_scr[lo:lo + length, :] = xc

        for n in range(tc // blk):
            cols = slice(n * blk, (n + 1) * blk)
            xc_n = xc_scr[rows, cols]
            gate = jnp.dot(xc_n.astype(BF16), gw_ref[n], preferred_element_type=F32) + gb_ref[n]
            gate = jax.nn.sigmoid(gate)
            log_a = (-RG_C * gate[:, :blk]) * softplus[:, cols]
            a_n = jnp.exp(log_a)
            a_scr[rows, cols] = a_n
            var = 1.0 - a_n * a_n
            mult = jnp.where(var > 0.0, var * lax.rsqrt(var), 0.0)
            b_scr[rows, cols] = mult * gate[:, blk:] * xc_n

        a = a_scr[rows, :].reshape(part // SUBLANES, SUBLANES, tc)
        b = b_scr[rows, :].reshape(part // SUBLANES, SUBLANES, tc)
        row = lax.broadcasted_iota(jnp.int32, a.shape, 1)
        shift = 1
        while shift < SUBLANES:
            keep = row >= shift
            a_prev = pltpu.roll(a, shift, axis=1)
            b_prev = pltpu.roll(b, shift, axis=1)
            b = jnp.where(keep, a * b_prev + b, b)
            a = jnp.where(keep, a * a_prev, a)
            shift *= 2
        a_scr[rows, :] = a.reshape(part, tc)
        b_scr[rows, :] = b.reshape(part, tc)

        for s, start, length in segs:
            lo = start if nseg == 1 else s * tseg
            if nseg > 1 or p == 0:
                h_run = h0_ref[s] if tiles_per_seq == 1 else hcar[c]

            def group(i, h_prev, lo=lo):
                off = pl.multiple_of(lo + i * SUBLANES, SUBLANES)
                h = a_scr[pl.ds(off, SUBLANES), :] * h_prev + b_scr[pl.ds(off, SUBLANES), :]
                b_scr[pl.ds(off, SUBLANES), :] = h
                return h[SUBLANES - 1:SUBLANES, :]

            h_run = lax.fori_loop(0, length // SUBLANES, group, h_run, unroll=8)
            if nseg > 1 or p == n_parts - 1:
                hl_ref[s] = h_run
                if tiles_per_seq > 1:
                    hcar[c] = h_run

        out_ref[rows, :] += jnp.dot((b_scr[rows, :] * gelu_gate[p]).astype(BF16), wo_ref[...],
                                    preferred_element_type=F32)

    for s in range(nseg):
        new_hist = rbuf[s, pad + tseg:SUBLANES + tseg, :]
        nh_ref[s] = new_hist
        if tiles_per_seq > 1:
            ccar[c] = new_hist


def _rg_layer(x, hist, h0, t, norm_g, w_in, conv_w, conv_b, gate_w, gate_b, log_lam, w_out):
    m, d = x.shape
    n_seq = m // t
    c_rnn = w_out.shape[0]
    conv_width = conv_w.shape[0]
    blk = c_rnn // RG_BLOCKS
    tc = min(RG_COL_TILE, c_rnn)
    n_c = c_rnn // tc
    assert n_c >= 2
    tseg, nseg, tps = _row_tiling(n_seq, t)
    tm = tseg * nseg
    kern = functools.partial(_rg_kernel, tseg=tseg, nseg=nseg, tiles_per_seq=tps, blk=blk,
                             part=_part_rows(tseg, nseg))
    seq_blk = lambda r, c: (r // tps, 0, c)
    out, new_hist, h_last = pl.pallas_call(
        kern,
        grid=(m // tm, n_c),
        in_specs=[
            pl.BlockSpec(memory_space=pl.ANY),
            pl.BlockSpec((nseg, conv_width - 1, tc), seq_blk),
            pl.BlockSpec((nseg, 1, tc), seq_blk),
            pl.BlockSpec((1, d), lambda r, c: (0, 0)),
            pl.BlockSpec((d, tc), lambda r, c: (0, c)),
            pl.BlockSpec((d, tc), lambda r, c: (0, n_c + c)),
            pl.BlockSpec((conv_width, tc), lambda r, c: (0, c)),
            pl.BlockSpec((1, tc), lambda r, c: (0, c)),
            pl.BlockSpec((tc // blk, blk, 2 * blk), lambda r, c: (c, 0, 0)),
            pl.BlockSpec((tc // blk, 1, 2 * blk), lambda r, c: (c, 0, 0)),
            pl.BlockSpec((1, tc), lambda r, c: (0, c)),
            pl.BlockSpec((tc, d), lambda r, c: (c, 0)),
        ],
        out_specs=[
            pl.BlockSpec((tm, d), lambda r, c: (r, 0)),
            pl.BlockSpec((nseg, conv_width - 1, tc), lambda r, c: (r, 0, c)),
            pl.BlockSpec((nseg, 1, tc), lambda r, c: (r, 0, c)),
        ],
        out_shape=[
            jax.ShapeDtypeStruct((m, d), F32),
            jax.ShapeDtypeStruct((n_seq * tps, conv_width - 1, c_rnn), F32),
            jax.ShapeDtypeStruct((n_seq * tps, 1, c_rnn), F32),
        ],
        scratch_shapes=[
            pltpu.VMEM((tm, d), F32),
            pltpu.SemaphoreType.DMA,
            pltpu.VMEM((tm, d), BF16),
            pltpu.VMEM((nseg, SUBLANES + tseg, tc), F32),
            pltpu.VMEM((tm, tc), F32),
            pltpu.VMEM((tm, tc), F32),
            pltpu.VMEM((tm, tc), F32),
            pltpu.VMEM((n_c, conv_width - 1, tc), F32),
            pltpu.VMEM((n_c, 1, tc), F32),
        ],
        compiler_params=_params(2),
        name="rg_layer",
    )(x, hist, h0, norm_g.reshape(1, d), w_in, w_in, conv_w, conv_b.reshape(1, c_rnn),
      gate_w, gate_b.reshape(RG_BLOCKS, 1, 2 * blk), log_lam.reshape(1, c_rnn), w_out)
    return out, _last_tile(new_hist, n_seq, tps), _last_tile(h_last, n_seq, tps)


def _ffn_kernel(x_hbm, hist_ref, g_ref, wg_ref, wv_ref, cwg_ref, cwv_ref, cbg_ref, cbv_ref, wd_ref,
                fg_ref, out_ref, nh_ref,
                x_buf, x_sem, xn_scr, ubuf, hid_scr, ccar,
                *, tseg, nseg, tiles_per_seq, final_norm, part):
    r = pl.program_id(0)
    c = pl.program_id(1)
    n_c = pl.num_programs(1)
    conv_w = cwg_ref.shape[0]
    tm = tseg * nseg
    tb = r % tiles_per_seq
    pad = SUBLANES - (conv_w - 1)
    halves = ((wg_ref, cwg_ref, cbg_ref), (wv_ref, cwv_ref, cbv_ref))

    def consume():
        x = x_buf[...]
        xn_scr[...] = _rmsnorm_rows(x, g_ref[...]).astype(BF16)
        out_ref[...] = x

    _stream_x_tile(x_hbm, x_buf, x_sem, consume)

    for s in range(nseg):
        for half in range(2):
            def load_state(half=half, s=s):
                for k in range(conv_w - 1):
                    ubuf[half, s, pad + k:pad + k + 1, :] = hist_ref[s, k, half:half + 1, :]

            if tiles_per_seq == 1:
                load_state()
            else:
                pl.when(tb == 0)(load_state)

                @pl.when(tb != 0)
                def _():
                    ubuf[half, s, pad:SUBLANES, :] = ccar[c, half]

    n_parts = tm // part
    for p in range(n_parts):
        xn = xn_scr[p * part:(p + 1) * part, :]
        for half, (w_ref, _, _) in enumerate(halves):
            u = jnp.dot(xn, w_ref[...], preferred_element_type=F32)
            if nseg == 1:
                ubuf[half, 0, SUBLANES + p * part:SUBLANES + (p + 1) * part, :] = u
            else:
                for s in range(nseg):
                    ubuf[half, s, SUBLANES:SUBLANES + tseg, :] = u[s * tseg:(s + 1) * tseg]

    for p in range(n_parts):
        rows = slice(p * part, (p + 1) * part)
        segs = [(0, p * part, part)] if nseg == 1 else [(s, 0, tseg) for s in range(nseg)]
        for s, start, length in segs:
            conv = []
            for half, (_, cw_ref, cb_ref) in enumerate(halves):
                cv = cb_ref[...] + ubuf[half, s, pad + start:pad + start + length, :] * cw_ref[0:1, :]
                for k in range(1, conv_w):
                    lo = pad + start + k
                    cv = cv + ubuf[half, s, lo:lo + length, :] * cw_ref[k:k + 1, :]
                conv.append(cv)
            hid = (jax.nn.gelu(conv[0]) * conv[1]).astype(BF16)
            if nseg > 1:
                hid_scr[s * tseg:(s + 1) * tseg, :] = hid
        if nseg > 1:
            hid = hid_scr[...]
        out_ref[rows, :] += jnp.dot(hid, wd_ref[...], preferred_element_type=F32)

    for s in range(nseg):
        for half in range(2):
            for k in range(conv_w - 1):
                nh_ref[s, k, half:half + 1, :] = ubuf[half, s, pad + tseg + k:pad + tseg + k + 1, :]
            if tiles_per_seq > 1:
                ccar[c, half] = ubuf[half, s, pad + tseg:SUBLANES + tseg, :]

    if final_norm:
        @pl.when(c == n_c - 1)
        def _():
            out_ref[...] = _rmsnorm_rows(out_ref[...], fg_ref[...])


def _ffn_layer(x, hist, t, norm_g, w_up, conv_w, conv_b, w_down, final_g):
    m, d = x.shape
    n_seq = m // t
    d_ff = w_down.shape[0]
    conv_width = conv_w.shape[0]
    tf = FF_COL_TILE
    assert d_ff % tf == 0
    n_c = d_ff // tf
    assert n_c >= 2
    tseg, nseg, tps = _row_tiling(n_seq, t)
    tm = tseg * nseg
    final_norm = final_g is not None
    if final_g is None:
        final_g = jnp.ones((d,), F32)
    kern = functools.partial(_ffn_kernel, tseg=tseg, nseg=nseg, tiles_per_seq=tps,
                             final_norm=final_norm, part=_part_rows(tseg, nseg))
    hist4 = hist.reshape(n_seq, conv_width - 1, 2, d_ff)
    conv_b2 = conv_b.reshape(1, 2 * d_ff)
    out, new_hist = pl.pallas_call(
        kern,
        grid=(m // tm, n_c),
        in_specs=[
            pl.BlockSpec(memory_space=pl.ANY),
            pl.BlockSpec((nseg, conv_width - 1, 2, tf), lambda r, c: (r // tps, 0, 0, c)),
            pl.BlockSpec((1, d), lambda r, c: (0, 0)),
            pl.BlockSpec((d, tf), lambda r, c: (0, c)),
            pl.BlockSpec((d, tf), lambda r, c: (0, n_c + c)),
            pl.BlockSpec((conv_width, tf), lambda r, c: (0, c)),
            pl.BlockSpec((conv_width, tf), lambda r, c: (0, n_c + c)),
            pl.BlockSpec((1, tf), lambda r, c: (0, c)),
            pl.BlockSpec((1, tf), lambda r, c: (0, n_c + c)),
            pl.BlockSpec((tf, d), lambda r, c: (c, 0)),
            pl.BlockSpec((1, d), lambda r, c: (0, 0)),
        ],
        out_specs=[
            pl.BlockSpec((tm, d), lambda r, c: (r, 0)),
            pl.BlockSpec((nseg, conv_width - 1, 2, tf), lambda r, c: (r, 0, 0, c)),
        ],
        out_shape=[
            jax.ShapeDtypeStruct((m, d), F32),
            jax.ShapeDtypeStruct((n_seq * tps, conv_width - 1, 2, d_ff), F32),
        ],
        scratch_shapes=[
            pltpu.VMEM((tm, d), F32),
            pltpu.SemaphoreType.DMA,
            pltpu.VMEM((tm, d), BF16),
            pltpu.VMEM((2, nseg, SUBLANES + tseg, tf), F32),
            pltpu.VMEM((tm if nseg > 1 else SUBLANES, tf), BF16),
            pltpu.VMEM((n_c, 2, conv_width - 1, tf), F32),
        ],
        compiler_params=_params(2),
        name="conv_ffn",
    )(x, hist4, norm_g.reshape(1, d), w_up, w_up, conv_w, conv_w, conv_b2, conv_b2, w_down,
      final_g.reshape(1, d))
    return out, _last_tile(new_hist, n_seq, tps).reshape(n_seq, conv_width - 1, 2 * d_ff)


def _q_kernel(x_ref, g_ref, w_ref, q_ref, xn_ref, *, scale):
    xn = _rmsnorm_rows(x_ref[...], g_ref[...]).astype(BF16)
    xn_ref[...] = xn
    q_ref[...] = (jnp.dot(xn, w_ref[...], preferred_element_type=F32) * scale).astype(BF16)


def _q_proj(x, norm_g, w_qkv, scale):
    m, d = x.shape
    d_att = w_qkv.shape[1] // 3
    tm = min(QKV_ROW_TILE, m)
    return pl.pallas_call(
        functools.partial(_q_kernel, scale=scale),
        grid=(m // tm,),
        in_specs=[
            pl.BlockSpec((tm, d), lambda r: (r, 0)),
            pl.BlockSpec((1, d), lambda r: (0, 0)),
            pl.BlockSpec((d, d_att), lambda r: (0, 0), pipeline_mode=pl.Buffered(1)),
        ],
        out_specs=[
            pl.BlockSpec((tm, d_att), lambda r: (r, 0)),
            pl.BlockSpec((tm, d), lambda r: (r, 0)),
        ],
        out_shape=[
            jax.ShapeDtypeStruct((m, d_att), BF16),
            jax.ShapeDtypeStruct((m, d), BF16),
        ],
        compiler_params=_params(1),
        name="q_proj",
    )(x, norm_g.reshape(1, d), w_qkv)


def _kv_kernel(xn_ref, w_ref, heads_ref, flat_ref):
    y = jnp.dot(xn_ref[...], w_ref[...], preferred_element_type=F32)
    flat_ref[...] = y.astype(BF16)
    heads_ref[...] = pltpu.einshape("m(hd)->mhd", y, h=heads_ref.shape[1])


def _kv_proj(xn, w_qkv, which, n_split):
    m, d = xn.shape
    d_att = w_qkv.shape[1] // 3
    tm = min(QKV_ROW_TILE, m)
    return pl.pallas_call(
        _kv_kernel,
        grid=(m // tm,),
        in_specs=[
            pl.BlockSpec((tm, d), lambda r: (r, 0)),
            pl.BlockSpec((d, d_att), lambda r: (0, which), pipeline_mode=pl.Buffered(1)),
        ],
        out_specs=[
            pl.BlockSpec((tm, n_split, d_att // n_split), lambda r: (r, 0, 0)),
            pl.BlockSpec((tm, d_att), lambda r: (r, 0)),
        ],
        out_shape=[
            jax.ShapeDtypeStruct((m, n_split, d_att // n_split), F32),
            jax.ShapeDtypeStruct((m, d_att), BF16),
        ],
        compiler_params=_params(1),
        name="kv_proj",
    )(xn, w_qkv)


def _diff_lambda(lam_ref, lambda_init):
    lam = lam_ref[...]
    s1 = jnp.sum(lam[0:1] * lam[1:2], axis=-1, keepdims=True)
    s2 = jnp.sum(lam[2:3] * lam[3:4], axis=-1, keepdims=True)
    return jnp.exp(s1) - jnp.exp(s2) + lambda_init


def _head_norm(o, subln_ref, lambda_init):
    return _rmsnorm_rows(o, subln_ref[...]) * (1.0 - lambda_init)


def _nt_dot(a, b):
    return lax.dot_general(a, b, (((1,), (1,)), ((), ())), preferred_element_type=F32)


def _attn_prompt_kernel(q_ref, k_ref, v_ref, lam_ref, subln_ref, o_ref, *, lambda_init, tq):
    t, hw = q_ref.shape
    hd = hw // 2
    lam = _diff_lambda(lam_ref, lambda_init)

    def fold(x, op):
        acc = x[:, 0:LANES]
        for g in range(1, x.shape[1] // LANES):
            acc = op(acc, x[:, g * LANES:(g + 1) * LANES])
        return acc

    def spread(x, width):
        return jnp.concatenate([x] * (width // LANES), axis=1)

    def row_max(s):
        return jnp.broadcast_to(jnp.max(fold(s, jnp.maximum), axis=1, keepdims=True),
                                (s.shape[0], LANES))

    q_chunk = lax.broadcasted_iota(jnp.int32, (tq, tq), 0) // CHUNK
    k_chunk = lax.broadcasted_iota(jnp.int32, (tq, tq), 1) // CHUNK
    visible = k_chunk <= q_chunk
    for i in range(t // tq):
        rows = slice(i * tq, (i + 1) * tq)
        outs = []
        for j in range(2):
            cols = slice(j * hd, (j + 1) * hd)
            q = q_ref[rows, cols]
            s = jnp.where(visible, _nt_dot(q, k_ref[rows, cols]), NEG_INF)
            m = row_max(s)
            p = jnp.exp(s - spread(m, tq))
            l = fold(p, jnp.add)
            acc = jnp.dot(p.astype(BF16), v_ref[rows, :], preferred_element_type=F32)
            for kb in range(i):
                keys = slice(kb * tq, (kb + 1) * tq)
                s = _nt_dot(q, k_ref[keys, cols])
                m_new = jnp.maximum(m, row_max(s))
                alpha = jnp.exp(m - m_new)
                p = jnp.exp(s - spread(m_new, tq))
                l = alpha * l + fold(p, jnp.add)
                acc = spread(alpha, hw) * acc + jnp.dot(p.astype(BF16), v_ref[keys, :],
                                                        preferred_element_type=F32)
                m = m_new
            outs.append(acc / jnp.sum(l, axis=1, keepdims=True))
        o = outs[0] - lam * outs[1]
        o_ref[rows, :] = _head_norm(o, subln_ref, lambda_init).astype(BF16)


def _attn_prompt(q, kb, vb, t, lam_p, subln, lambda_init):
    m, d_att = q.shape
    n_seq = m // t
    hw = d_att // N_HEADS
    tq = min(ATTN_TILE, t)
    assert t % tq == 0 and tq % CHUNK == 0
    seq_head = pl.BlockSpec((t, hw), lambda b, h: (b, h))
    return pl.pallas_call(
        functools.partial(_attn_prompt_kernel, lambda_init=lambda_init, tq=tq),
        grid=(n_seq, N_HEADS),
        in_specs=[
            seq_head, seq_head, seq_head,
            pl.BlockSpec(lam_p.shape, lambda b, h: (0, 0)),
            pl.BlockSpec((1, hw), lambda b, h: (0, 0)),
        ],
        out_specs=seq_head,
        out_shape=jax.ShapeDtypeStruct((m, d_att), BF16),
        compiler_params=_params(2),
        name="attn_prompt",
    )(q, kb, vb, lam_p, subln.reshape(1, hw))


def _attn_sample_kernel(q_ref, ck_ref, cv_ref, kn_ref, vn_ref, lam_ref, subln_ref, o_ref,
                        m_scr, l_scr, acc_scr, *, lambda_init):
    c = pl.program_id(1)
    n_c = pl.num_programs(1)
    t, d_att = q_ref.shape
    hw = d_att // N_HEADS
    hd = hw // 2
    q = q_ref[...]

    @pl.when(c == 0)
    def _():
        m_scr[...] = jnp.full(m_scr.shape, NEG_INF, F32)
        l_scr[...] = jnp.zeros_like(l_scr)
        acc_scr[...] = jnp.zeros_like(acc_scr)

    def attend(k_all, v_all):
        n = k_all.shape[0]
        for h in range(N_HEADS):
            v_h = v_all[:, h * hw:(h + 1) * hw]
            for j in range(2):
                hm = 2 * h + j
                cols = slice(h * hw + j * hd, h * hw + (j + 1) * hd)
                s = _nt_dot(q[:, cols], k_all[:, cols])
                m_old = m_scr[hm]
                m_new = jnp.maximum(m_old, jnp.max(s, axis=1, keepdims=True))
                alpha = jnp.exp(m_old - m_new)
                p = jnp.exp(s - m_new[:, 0:1])
                if n % LANES == 0:
                    psum = p[:, 0:LANES]
                    for g in range(1, n // LANES):
                        psum = psum + p[:, g * LANES:(g + 1) * LANES]
                else:
                    psum = jnp.sum(p, axis=1, keepdims=True) * (1.0 / LANES)
                m_scr[hm] = m_new
                l_scr[hm] = alpha * l_scr[hm] + psum
                acc_scr[hm] = (jnp.concatenate([alpha] * (hw // LANES), axis=1) * acc_scr[hm]
                               + jnp.dot(p.astype(BF16), v_h, preferred_element_type=F32))

    attend(pltpu.einshape("mhd->m(hd)", ck_ref[0, 0]).astype(BF16),
           pltpu.einshape("mhd->m(hd)", cv_ref[0, 0]).astype(BF16))

    @pl.when(c == n_c - 1)
    def _():
        attend(kn_ref[...], vn_ref[...])
        lam = _diff_lambda(lam_ref, lambda_init)
        for h in range(N_HEADS):
            outs = [acc_scr[2 * h + j] / jnp.sum(l_scr[2 * h + j], axis=1, keepdims=True)
                    for j in range(2)]
            o = outs[0] - lam * outs[1]
            o_ref[:, h * hw:(h + 1) * hw] = _head_norm(o, subln_ref, lambda_init).astype(BF16)


def _attn_sample(q, cache_k, cache_v, layer, kb, vb, t, lam_p, subln, lambda_init):
    m, d_att = q.shape
    n_seq = m // t
    past = cache_k.shape[2]
    hw = d_att // N_HEADS
    pc = min(SAMPLE_KEY_TILE, past)
    assert past % pc == 0
    new_blk = pl.BlockSpec((t, d_att), lambda b, c: (b, 0))
    cache_blk = lambda arr: pl.BlockSpec((1, 1, pc) + arr.shape[3:], lambda b, c: (layer, b, c, 0, 0))
    return pl.pallas_call(
        functools.partial(_attn_sample_kernel, lambda_init=lambda_init),
        grid=(n_seq, past // pc),
        in_specs=[
            new_blk, cache_blk(cache_k), cache_blk(cache_v), new_blk, new_blk,
            pl.BlockSpec(lam_p.shape, lambda b, c: (0, 0)),
            pl.BlockSpec((1, hw), lambda b, c: (0, 0)),
        ],
        out_specs=new_blk,
        out_shape=jax.ShapeDtypeStruct((m, d_att), BF16),
        scratch_shapes=[
            pltpu.VMEM((2 * N_HEADS, t, LANES), F32),
            pltpu.VMEM((2 * N_HEADS, t, LANES), F32),
            pltpu.VMEM((2 * N_HEADS, t, hw), F32),
        ],
        compiler_params=_params(2),
        name="attn_sample",
    )(q, cache_k, cache_v, kb, vb, lam_p, subln.reshape(1, hw))


def _proj_residual_kernel(o_ref, w_ref, x_ref, out_ref):
    out_ref[...] = x_ref[...] + jnp.dot(o_ref[...], w_ref[...], preferred_element_type=F32)


def _proj_residual(o, w, x):
    m, kdim = o.shape
    n = w.shape[1]
    tm = min(PROJ_ROW_TILE, m)
    tn = PROJ_COL_TILE
    return pl.pallas_call(
        _proj_residual_kernel,
        grid=(m // tm, n // tn),
        in_specs=[
            pl.BlockSpec((tm, kdim), lambda r, c: (r, 0)),
            pl.BlockSpec((kdim, tn), lambda r, c: (0, c)),
            pl.BlockSpec((tm, tn), lambda r, c: (r, c)),
        ],
        out_specs=pl.BlockSpec((tm, tn), lambda r, c: (r, c)),
        out_shape=jax.ShapeDtypeStruct((m, n), F32),
        compiler_params=_params(2),
        name="attn_out_proj",
    )(o, w, x)


def kernel(x_prompt, x_sample, state_rglru_conv, state_rglru_h, cache_attn_k, cache_attn_v, state_ffn_conv, rg_norm, rg_w_in, rg_conv_w, rg_conv_b, rg_gate_w, rg_gate_b, rg_log_lambda, rg_w_out, at_norm, at_w_qkv, at_lambda, at_subln, at_w_out, ffn_norm, ffn_w_up, ffn_conv_w, ffn_conv_b, ffn_w_down, final_norm):
    bp, tp, d = x_prompt.shape
    bs, ts, _ = x_sample.shape
    depth = ffn_norm.shape[0]
    c_rnn = rg_w_out.shape[1]
    d_att = at_w_out.shape[1]
    head_dim = d_att // (2 * N_HEADS)

    xp = x_prompt.reshape(bp * tp, d)
    xs = x_sample.reshape(bs * ts, d)
    p_rg_conv, p_rg_h, p_k, p_v, p_ffn = [], [], [], [], []
    s_rg_conv, s_rg_h, s_k, s_v, s_ffn = [], [], [], [], []
    for layer in range(depth):
        j = layer // 2
        if layer % 2 == 0:
            w = (rg_norm[j], rg_w_in[j].astype(BF16), rg_conv_w[j], rg_conv_b[j],
                 rg_gate_w[j].astype(BF16), rg_gate_b[j], rg_log_lambda[j], rg_w_out[j].astype(BF16))
            zero_hist = jnp.zeros((bp, rg_conv_w.shape[1] - 1, c_rnn), F32)
            zero_h = jnp.zeros((bp, 1, c_rnn), F32)
            xp, cp, hp = _rg_layer(xp, zero_hist, zero_h, tp, *w)
            xs, cs, hs = _rg_layer(xs, state_rglru_conv[j], state_rglru_h[j].reshape(bs, 1, c_rnn),
                                   ts, *w)
            p_rg_conv.append(cp); p_rg_h.append(hp.reshape(bp, c_rnn))
            s_rg_conv.append(cs); s_rg_h.append(hs.reshape(bs, c_rnn))
        else:
            lambda_init = 0.8 - 0.6 * math.exp(-0.3 * layer)
            w_qkv = at_w_qkv[j].astype(BF16)
            w_out = at_w_out[j].astype(BF16)
            scale = head_dim ** -0.5
            qp, xnp = _q_proj(xp, at_norm[j], w_qkv, scale)
            kp, kpb = _kv_proj(xnp, w_qkv, 1, 2 * N_HEADS)
            vp, vpb = _kv_proj(xnp, w_qkv, 2, N_HEADS)
            op = _attn_prompt(qp, kpb, vpb, tp, at_lambda[j], at_subln[j], lambda_init)
            xp = _proj_residual(op, w_out, xp)
            qs, xns = _q_proj(xs, at_norm[j], w_qkv, scale)
            kn, knb = _kv_proj(xns, w_qkv, 1, 2 * N_HEADS)
            vn, vnb = _kv_proj(xns, w_qkv, 2, N_HEADS)
            os_ = _attn_sample(qs, cache_attn_k, cache_attn_v, j, knb, vnb, ts,
                               at_lambda[j], at_subln[j], lambda_init)
            xs = _proj_residual(os_, w_out, xs)
            p_k.append(kp.reshape(bp, tp, 2 * N_HEADS, head_dim))
            p_v.append(vp.reshape(bp, tp, N_HEADS, 2 * head_dim))
            s_k.append(kn.reshape(bs, ts, 2 * N_HEADS, head_dim))
            s_v.append(vn.reshape(bs, ts, N_HEADS, 2 * head_dim))
        last = layer == depth - 1
        fw = (ffn_norm[layer], ffn_w_up[layer].astype(BF16), ffn_conv_w[layer], ffn_conv_b[layer],
              ffn_w_down[layer].astype(BF16), final_norm if last else None)
        zero_ffn = jnp.zeros((bp,) + state_ffn_conv.shape[2:], F32)
        xp, fcp = _ffn_layer(xp, zero_ffn, tp, *fw)
        xs, fcs = _ffn_layer(xs, state_ffn_conv[layer], ts, *fw)
        p_ffn.append(fcp); s_ffn.append(fcs)
    stack = lambda arrs: arrs[0][None] if len(arrs) == 1 else jnp.stack(arrs)
    return (xp.reshape(bp, tp, d), xs.reshape(bs, ts, d),
            stack(p_rg_conv), stack(p_rg_h), stack(p_k), stack(p_v), stack(p_ffn),
            stack(s_rg_conv), stack(s_rg_h), stack(s_k), stack(s_v), stack(s_ffn))
```

```python
import functools
import math

import jax
import jax.numpy as jnp
from jax import lax
from jax.experimental import pallas as pl
from jax.experimental.pallas import tpu as pltpu

F32 = jnp.float32
BF16 = jnp.bfloat16

EPS = 1e-6
NEG_INF = -1e30
CHUNK = 64
N_HEADS = 8
RG_BLOCKS = 16
RG_C = 8.0
SUBLANES = 8
LANES = 128
VMEM_LIMIT_BYTES = 58 * 1024 * 1024

LAYER_ROW_TILE = 1024
PART_ROWS = 512
QKV_ROW_TILE = 512
RG_COL_TILE = 512
FF_COL_TILE = 512
ATTN_TILE = 512
SAMPLE_KEY_TILE = 512


def _params(n_axes):
    return pltpu.CompilerParams(
        dimension_semantics=("arbitrary",) * n_axes,
        vmem_limit_bytes=VMEM_LIMIT_BYTES,
    )


def _rmsnorm_rows(x, g):
    ms = jnp.mean(x * x, axis=-1, keepdims=True)
    return (x * lax.rsqrt(ms + EPS)) * g


def _row_tiling(n_seq, t):
    if t >= LAYER_ROW_TILE:
        assert t % LAYER_ROW_TILE == 0
        return LAYER_ROW_TILE, 1, t // LAYER_ROW_TILE
    assert LAYER_ROW_TILE % t == 0 and t % SUBLANES == 0
    nseg = min(LAYER_ROW_TILE // t, n_seq)
    assert n_seq % nseg == 0
    return t, nseg, 1


def _part_rows(tseg, nseg):
    tm = tseg * nseg
    if nseg == 1 and tm % PART_ROWS == 0:
        return PART_ROWS
    return tm


def _last_tile(state, n_seq, tiles_per_seq):
    if tiles_per_seq == 1:
        return state
    return state.reshape((n_seq, tiles_per_seq) + state.shape[1:])[:, -1]


def _x_tile_copy(x_hbm, x_buf, sem, r):
    tm = x_buf.shape[0]
    return pltpu.make_async_copy(x_hbm.at[pl.ds(pl.multiple_of(r * tm, tm), tm)], x_buf, sem)


def _stream_x_tile(x_hbm, x_buf, sem, consume):
    r = pl.program_id(0)
    c = pl.program_id(1)
    n_r = pl.num_programs(0)

    @pl.when(c == 0)
    def _():
        @pl.when(r == 0)
        def _():
            _x_tile_copy(x_hbm, x_buf, sem, r).start()

        _x_tile_copy(x_hbm, x_buf, sem, r).wait()
        consume()

    @pl.when(jnp.logical_and(c == 1, r + 1 < n_r))
    def _():
        _x_tile_copy(x_hbm, x_buf, sem, r + 1).start()


def _rg_kernel(x_hbm, hist_ref, h0_ref, g_ref, wg_ref, wr_ref, cw_ref, cb_ref, gw_ref, gb_ref,
               ll_ref, wo_ref,
               out_ref, nh_ref, hl_ref,
               x_buf, x_sem, xn_scr, rbuf, xc_scr, a_scr, b_scr, ccar, hcar,
               *, tseg, nseg, tiles_per_seq, blk, part):
    r = pl.program_id(0)
    c = pl.program_id(1)
    conv_w = cw_ref.shape[0]
    tc = wg_ref.shape[1]
    tm = tseg * nseg
    tb = r % tiles_per_seq
    pad = SUBLANES - (conv_w - 1)

    def consume():
        x = x_buf[...]
        xn_scr[...] = _rmsnorm_rows(x, g_ref[...]).astype(BF16)
        out_ref[...] = x

    _stream_x_tile(x_hbm, x_buf, x_sem, consume)

    for s in range(nseg):
        if tiles_per_seq == 1:
            rbuf[s, pad:SUBLANES, :] = hist_ref[s]
        else:
            @pl.when(tb == 0)
            def _():
                rbuf[s, pad:SUBLANES, :] = hist_ref[s]
                hcar[c] = h0_ref[s]

            @pl.when(tb != 0)
            def _():
                rbuf[s, pad:SUBLANES, :] = ccar[c]

    z = -ll_ref[...]
    softplus = jnp.maximum(z, 0.0) + jnp.log1p(jnp.exp(-jnp.abs(z)))

    n_parts = tm // part
    gelu_gate = []
    for p in range(n_parts):
        xn = xn_scr[p * part:(p + 1) * part, :]
        gelu_gate.append(jax.nn.gelu(jnp.dot(xn, wg_ref[...], preferred_element_type=F32)))
        rec = jnp.dot(xn, wr_ref[...], preferred_element_type=F32)
        if nseg == 1:
            rbuf[0, SUBLANES + p * part:SUBLANES + (p + 1) * part, :] = rec
        else:
            for s in range(nseg):
                rbuf[s, SUBLANES:SUBLANES + tseg, :] = rec[s * tseg:(s + 1) * tseg]

    h_run = None
    for p in range(n_parts):
        rows = slice(p * part, (p + 1) * part)

        segs = [(0, p * part, part)] if nseg == 1 else [(s, 0, tseg) for s in range(nseg)]
        for s, start, length in segs:
            xc = cb_ref[...] + rbuf[s, pad + start:pad + start + length, :] * cw_ref[0:1, :]
            for k in range(1, conv_w):
                lo = pad + start + k
                xc = xc + rbuf[s, lo:lo + length, :] * cw_ref[k:k + 1, :]
            lo = start if nseg == 1 else s * tseg
            xc_scr[lo:lo + length, :] = xc

        for n in range(tc // blk):
            cols = slice(n * blk, (n + 1) * blk)
            xc_n = xc_scr[rows, cols]
            gate = jnp.dot(xc_n.astype(BF16), gw_ref[n], preferred_element_type=F32) + gb_ref[n]
            gate = jax.nn.sigmoid(gate)
            log_a = (-RG_C * gate[:, :blk]) * softplus[:, cols]
            a_n = jnp.exp(log_a)
            a_scr[rows, cols] = a_n
            var = 1.0 - a_n * a_n
            mult = jnp.where(var > 0.0, var * lax.rsqrt(var), 0.0)
            b_scr[rows, cols] = mult * gate[:, blk:] * xc_n

        a = a_scr[rows, :].reshape(part // SUBLANES, SUBLANES, tc)
        b = b_scr[rows, :].reshape(part // SUBLANES, SUBLANES, tc)
        row = lax.broadcasted_iota(jnp.int32, a.shape, 1)
        shift = 1
        while shift < SUBLANES:
            keep = row >= shift
            a_prev = pltpu.roll(a, shift, axis=1)
            b_prev = pltpu.roll(b, shift, axis=1)
            b = jnp.where(keep, a * b_prev + b, b)
            a = jnp.where(keep, a * a_prev, a)
            shift *= 2
        a_scr[rows, :] = a.reshape(part, tc)
        b_scr[rows, :] = b.reshape(part, tc)

        for s, start, length in segs:
            lo = start if nseg == 1 else s * tseg
            if nseg > 1 or p == 0:
                h_run = h0_ref[s] if tiles_per_seq == 1 else hcar[c]

            for i in range(length // SUBLANES):
                off = lo + i * SUBLANES
                h = a_scr[off:off + SUBLANES, :] * h_run + b_scr[off:off + SUBLANES, :]
                b_scr[off:off + SUBLANES, :] = h
                h_run = h[SUBLANES - 1:SUBLANES, :]
            if nseg > 1 or p == n_parts - 1:
                hl_ref[s] = h_run
                if tiles_per_seq > 1:
                    hcar[c] = h_run

        out_ref[rows, :] += jnp.dot((b_scr[rows, :] * gelu_gate[p]).astype(BF16), wo_ref[...],
                                    preferred_element_type=F32)

    for s in range(nseg):
        new_hist = rbuf[s, pad + tseg:SUBLANES + tseg, :]
        nh_ref[s] = new_hist
        if tiles_per_seq > 1:
            ccar[c] = new_hist


def _rg_layer(x, hist, h0, t, norm_g, w_in, conv_w, conv_b, gate_w, gate_b, log_lam, w_out):
    m, d = x.shape
    n_seq = m // t
    c_rnn = w_out.shape[0]
    conv_width = conv_w.shape[0]
    blk = c_rnn // RG_BLOCKS
    tc = min(RG_COL_TILE, c_rnn)
    n_c = c_rnn // tc
    assert n_c >= 2
    tseg, nseg, tps = _row_tiling(n_seq, t)
    tm = tseg * nseg
    kern = functools.partial(_rg_kernel, tseg=tseg, nseg=nseg, tiles_per_seq=tps, blk=blk,
                             part=_part_rows(tseg, nseg))
    seq_blk = lambda r, c: (r // tps, 0, c)
    out, new_hist, h_last = pl.pallas_call(
        kern,
        grid=(m // tm, n_c),
        in_specs=[
            pl.BlockSpec(memory_space=pl.ANY),
            pl.BlockSpec((nseg, conv_width - 1, tc), seq_blk),
            pl.BlockSpec((nseg, 1, tc), seq_blk),
            pl.BlockSpec((1, d), lambda r, c: (0, 0)),
            pl.BlockSpec((d, tc), lambda r, c: (0, c)),
            pl.BlockSpec((d, tc), lambda r, c: (0, n_c + c)),
            pl.BlockSpec((conv_width, tc), lambda r, c: (0, c)),
            pl.BlockSpec((1, tc), lambda r, c: (0, c)),
            pl.BlockSpec((tc // blk, blk, 2 * blk), lambda r, c: (c, 0, 0)),
            pl.BlockSpec((tc // blk, 1, 2 * blk), lambda r, c: (c, 0, 0)),
            pl.BlockSpec((1, tc), lambda r, c: (0, c)),
            pl.BlockSpec((tc, d), lambda r, c: (c, 0)),
        ],
        out_specs=[
            pl.BlockSpec((tm, d), lambda r, c: (r, 0)),
            pl.BlockSpec((nseg, conv_width - 1, tc), lambda r, c: (r, 0, c)),
            pl.BlockSpec((nseg, 1, tc), lambda r, c: (r, 0, c)),
        ],
        out_shape=[
            jax.ShapeDtypeStruct((m, d), F32),
            jax.ShapeDtypeStruct((n_seq * tps, conv_width - 1, c_rnn), F32),
            jax.ShapeDtypeStruct((n_seq * tps, 1, c_rnn), F32),
        ],
        scratch_shapes=[
            pltpu.VMEM((tm, d), F32),
            pltpu.SemaphoreType.DMA,
            pltpu.VMEM((tm, d), BF16),
            pltpu.VMEM((nseg, SUBLANES + tseg, tc), F32),
            pltpu.VMEM((tm, tc), F32),
            pltpu.VMEM((tm, tc), F32),
            pltpu.VMEM((tm, tc), F32),
            pltpu.VMEM((n_c, conv_width - 1, tc), F32),
            pltpu.VMEM((n_c, 1, tc), F32),
        ],
        compiler_params=_params(2),
        name="rg_layer",
    )(x, hist, h0, norm_g.reshape(1, d), w_in, w_in, conv_w, conv_b.reshape(1, c_rnn),
      gate_w, gate_b.reshape(RG_BLOCKS, 1, 2 * blk), log_lam.reshape(1, c_rnn), w_out)
    return out, _last_tile(new_hist, n_seq, tps), _last_tile(h_last, n_seq, tps)


def _ffn_kernel(x_hbm, hist_ref, g_ref, wg_ref, wv_ref, cwg_ref, cwv_ref, cbg_ref, cbv_ref, wd_ref,
                fg_ref, out_ref, nh_ref,
                x_buf, x_sem, xn_scr, ubuf, hid_scr, ccar,
                *, tseg, nseg, tiles_per_seq, final_norm, part):
    r = pl.program_id(0)
    c = pl.program_id(1)
    n_c = pl.num_programs(1)
    conv_w = cwg_ref.shape[0]
    tm = tseg * nseg
    tb = r % tiles_per_seq
    pad = SUBLANES - (conv_w - 1)
    halves = ((wg_ref, cwg_ref, cbg_ref), (wv_ref, cwv_ref, cbv_ref))

    def consume():
        x = x_buf[...]
        xn_scr[...] = _rmsnorm_rows(x, g_ref[...]).astype(BF16)
        out_ref[...] = x

    _stream_x_tile(x_hbm, x_buf, x_sem, consume)

    for s in range(nseg):
        for half in range(2):
            def load_state(half=half, s=s):
                for k in range(conv_w - 1):
                    ubuf[half, s, pad + k:pad + k + 1, :] = hist_ref[s, k, half:half + 1, :]

            if tiles_per_seq == 1:
                load_state()
            else:
                pl.when(tb == 0)(load_state)

                @pl.when(tb != 0)
                def _():
                    ubuf[half, s, pad:SUBLANES, :] = ccar[c, half]

    n_parts = tm // part
    for p in range(n_parts):
        xn = xn_scr[p * part:(p + 1) * part, :]
        for half, (w_ref, _, _) in enumerate(halves):
            u = jnp.dot(xn, w_ref[...], preferred_element_type=F32)
            if nseg == 1:
                ubuf[half, 0, SUBLANES + p * part:SUBLANES + (p + 1) * part, :] = u
            else:
                for s in range(nseg):
                    ubuf[half, s, SUBLANES:SUBLANES + tseg, :] = u[s * tseg:(s + 1) * tseg]

    for p in range(n_parts):
        rows = slice(p * part, (p + 1) * part)
        segs = [(0, p * part, part)] if nseg == 1 else [(s, 0, tseg) for s in range(nseg)]
        for s, start, length in segs:
            conv = []
            for half, (_, cw_ref, cb_ref) in enumerate(halves):
                cv = cb_ref[...] + ubuf[half, s, pad + start:pad + start + length, :] * cw_ref[0:1, :]
                for k in range(1, conv_w):
                    lo = pad + start + k
                    cv = cv + ubuf[half, s, lo:lo + length, :] * cw_ref[k:k + 1, :]
                conv.append(cv)
            hid = (jax.nn.gelu(conv[0]) * conv[1]).astype(BF16)
            if nseg > 1:
                hid_scr[s * tseg:(s + 1) * tseg, :] = hid
        if nseg > 1:
            hid = hid_scr[...]
        out_ref[rows, :] += jnp.dot(hid, wd_ref[...], preferred_element_type=F32)

    for s in range(nseg):
        for half in range(2):
            for k in range(conv_w - 1):
                nh_ref[s, k, half:half + 1, :] = ubuf[half, s, pad + tseg + k:pad + tseg + k + 1, :]
            if tiles_per_seq > 1:
                ccar[c, half] = ubuf[half, s, pad + tseg:SUBLANES + tseg, :]

    if final_norm:
        @pl.when(c == n_c - 1)
        def _():
            out_ref[...] = _rmsnorm_rows(out_ref[...], fg_ref[...])


def _ffn_layer(x, hist, t, norm_g, w_up, conv_w, conv_b, w_down, final_g):
    m, d = x.shape
    n_seq = m // t
    d_ff = w_down.shape[0]
    conv_width = conv_w.shape[0]
    tf = FF_COL_TILE
    assert d_ff % tf == 0
    n_c = d_ff // tf
    assert n_c >= 2
    tseg, nseg, tps = _row_tiling(n_seq, t)
    tm = tseg * nseg
    final_norm = final_g is not None
    if final_g is None:
        final_g = jnp.ones((d,), F32)
    kern = functools.partial(_ffn_kernel, tseg=tseg, nseg=nseg, tiles_per_seq=tps,
                             final_norm=final_norm, part=_part_rows(tseg, nseg))
    hist4 = hist.reshape(n_seq, conv_width - 1, 2, d_ff)
    conv_b2 = conv_b.reshape(1, 2 * d_ff)
    out, new_hist = pl.pallas_call(
        kern,
        grid=(m // tm, n_c),
        in_specs=[
            pl.BlockSpec(memory_space=pl.ANY),
            pl.BlockSpec((nseg, conv_width - 1, 2, tf), lambda r, c: (r // tps, 0, 0, c)),
            pl.BlockSpec((1, d), lambda r, c: (0, 0)),
            pl.BlockSpec((d, tf), lambda r, c: (0, c)),
            pl.BlockSpec((d, tf), lambda r, c: (0, n_c + c)),
            pl.BlockSpec((conv_width, tf), lambda r, c: (0, c)),
            pl.BlockSpec((conv_width, tf), lambda r, c: (0, n_c + c)),
            pl.BlockSpec((1, tf), lambda r, c: (0, c)),
            pl.BlockSpec((1, tf), lambda r, c: (0, n_c + c)),
            pl.BlockSpec((tf, d), lambda r, c: (c, 0)),
            pl.BlockSpec((1, d), lambda r, c: (0, 0)),
        ],
        out_specs=[
            pl.BlockSpec((tm, d), lambda r, c: (r, 0)),
            pl.BlockSpec((nseg, conv_width - 1, 2, tf), lambda r, c: (r, 0, 0, c)),
        ],
        out_shape=[
            jax.ShapeDtypeStruct((m, d), F32),
            jax.ShapeDtypeStruct((n_seq * tps, conv_width - 1, 2, d_ff), F32),
        ],
        scratch_shapes=[
            pltpu.VMEM((tm, d), F32),
            pltpu.SemaphoreType.DMA,
            pltpu.VMEM((tm, d), BF16),
            pltpu.VMEM((2, nseg, SUBLANES + tseg, tf), F32),
            pltpu.VMEM((tm if nseg > 1 else SUBLANES, tf), BF16),
            pltpu.VMEM((n_c, 2, conv_width - 1, tf), F32),
        ],
        compiler_params=_params(2),
        name="conv_ffn",
    )(x, hist4, norm_g.reshape(1, d), w_up, w_up, conv_w, conv_w, conv_b2, conv_b2, w_down,
      final_g.reshape(1, d))
    return out, _last_tile(new_hist, n_seq, tps).reshape(n_seq, conv_width - 1, 2 * d_ff)


def _q_kernel(x_ref, g_ref, w_ref, q_ref, xn_ref, *, scale):
    xn = _rmsnorm_rows(x_ref[...], g_ref[...]).astype(BF16)
    xn_ref[...] = xn
    q_ref[...] = (jnp.dot(xn, w_ref[...], preferred_element_type=F32) * scale).astype(BF16)


def _q_proj(x, norm_g, w_qkv, scale):
    m, d = x.shape
    d_att = w_qkv.shape[1] // 3
    tm = min(QKV_ROW_TILE, m)
    return pl.pallas_call(
        functools.partial(_q_kernel, scale=scale),
        grid=(m // tm,),
        in_specs=[
            pl.BlockSpec((tm, d), lambda r: (r, 0)),
            pl.BlockSpec((1, d), lambda r: (0, 0)),
            pl.BlockSpec((d, d_att), lambda r: (0, 0), pipeline_mode=pl.Buffered(1)),
        ],
        out_specs=[
            pl.BlockSpec((tm, d_att), lambda r: (r, 0)),
            pl.BlockSpec((tm, d), lambda r: (r, 0)),
        ],
        out_shape=[
            jax.ShapeDtypeStruct((m, d_att), BF16),
            jax.ShapeDtypeStruct((m, d), BF16),
        ],
        compiler_params=_params(1),
        name="q_proj",
    )(x, norm_g.reshape(1, d), w_qkv)


def _kv_kernel(xn_ref, w_ref, heads_ref, flat_ref):
    y = jnp.dot(xn_ref[...], w_ref[...], preferred_element_type=F32)
    flat_ref[...] = y.astype(BF16)
    heads_ref[...] = pltpu.einshape("m(hd)->mhd", y, h=heads_ref.shape[1])


def _kv_proj(xn, w_qkv, which, n_split):
    m, d = xn.shape
    d_att = w_qkv.shape[1] // 3
    tm = min(QKV_ROW_TILE, m)
    return pl.pallas_call(
        _kv_kernel,
        grid=(m // tm,),
        in_specs=[
            pl.BlockSpec((tm, d), lambda r: (r, 0)),
            pl.BlockSpec((d, d_att), lambda r: (0, which), pipeline_mode=pl.Buffered(1)),
        ],
        out_specs=[
            pl.BlockSpec((tm, n_split, d_att // n_split), lambda r: (r, 0, 0)),
            pl.BlockSpec((tm, d_att), lambda r: (r, 0)),
        ],
        out_shape=[
            jax.ShapeDtypeStruct((m, n_split, d_att // n_split), F32),
            jax.ShapeDtypeStruct((m, d_att), BF16),
        ],
        compiler_params=_params(1),
        name="kv_proj",
    )(xn, w_qkv)


def _diff_lambda(lam_ref, lambda_init):
    lam = lam_ref[...]
    s1 = jnp.sum(lam[0:1] * lam[1:2], axis=-1, keepdims=True)
    s2 = jnp.sum(lam[2:3] * lam[3:4], axis=-1, keepdims=True)
    return jnp.exp(s1) - jnp.exp(s2) + lambda_init


def _head_norm(o, subln_ref, lambda_init):
    return _rmsnorm_rows(o, subln_ref[...]) * (1.0 - lambda_init)


def _nt_dot(a, b):
    return lax.dot_general(a, b, (((1,), (1,)), ((), ())), preferred_element_type=F32)


def _attn_prompt_kernel(q_ref, k_ref, v_ref, lam_ref, subln_ref, o_ref, *, lambda_init, tq):
    t, hw = q_ref.shape
    hd = hw // 2
    lam = _diff_lambda(lam_ref, lambda_init)

    def fold(x, op):
        acc = x[:, 0:LANES]
        for g in range(1, x.shape[1] // LANES):
            acc = op(acc, x[:, g * LANES:(g + 1) * LANES])
        return acc

    def spread(x, width):
        return jnp.concatenate([x] * (width // LANES), axis=1)

    def row_max(s):
        return jnp.broadcast_to(jnp.max(fold(s, jnp.maximum), axis=1, keepdims=True),
                                (s.shape[0], LANES))

    q_chunk = lax.broadcasted_iota(jnp.int32, (tq, tq), 0) // CHUNK
    k_chunk = lax.broadcasted_iota(jnp.int32, (tq, tq), 1) // CHUNK
    visible = k_chunk <= q_chunk
    for i in range(t // tq):
        rows = slice(i * tq, (i + 1) * tq)
        outs = []
        for j in range(2):
            cols = slice(j * hd, (j + 1) * hd)
            q = q_ref[rows, cols]
            s = jnp.where(visible, _nt_dot(q, k_ref[rows, cols]), NEG_INF)
            m = row_max(s)
            p = jnp.exp(s - spread(m, tq))
            l = fold(p, jnp.add)
            acc = jnp.dot(p.astype(BF16), v_ref[rows, :], preferred_element_type=F32)
            for kb in range(i):
                keys = slice(kb * tq, (kb + 1) * tq)
                s = _nt_dot(q, k_ref[keys, cols])
                m_new = jnp.maximum(m, row_max(s))
                alpha = jnp.exp(m - m_new)
                p = jnp.exp(s - spread(m_new, tq))
                l = alpha * l + fold(p, jnp.add)
                acc = spread(alpha, hw) * acc + jnp.dot(p.astype(BF16), v_ref[keys, :],
                                                        preferred_element_type=F32)
                m = m_new
            outs.append(acc / jnp.sum(l, axis=1, keepdims=True))
        o = outs[0] - lam * outs[1]
        o_ref[rows, :] = _head_norm(o, subln_ref, lambda_init).astype(BF16)


def _attn_prompt(q, kb, vb, t, lam_p, subln, lambda_init):
    m, d_att = q.shape
    n_seq = m // t
    hw = d_att // N_HEADS
    tq = min(ATTN_TILE, t)
    assert t % tq == 0 and tq % CHUNK == 0
    seq_head = pl.BlockSpec((t, hw), lambda b, h: (b, h))
    return pl.pallas_call(
        functools.partial(_attn_prompt_kernel, lambda_init=lambda_init, tq=tq),
        grid=(n_seq, N_HEADS),
        in_specs=[
            seq_head, seq_head, seq_head,
            pl.BlockSpec(lam_p.shape, lambda b, h: (0, 0)),
            pl.BlockSpec((1, hw), lambda b, h: (0, 0)),
        ],
        out_specs=seq_head,
        out_shape=jax.ShapeDtypeStruct((m, d_att), BF16),
        compiler_params=_params(2),
        name="attn_prompt",
    )(q, kb, vb, lam_p, subln.reshape(1, hw))


def _attn_sample_kernel(q_ref, ck_ref, cv_ref, kn_ref, vn_ref, lam_ref, subln_ref, o_ref,
                        m_scr, l_scr, acc_scr, *, lambda_init):
    c = pl.program_id(1)
    n_c = pl.num_programs(1)
    t, d_att = q_ref.shape
    hw = d_att // N_HEADS
    hd = hw // 2
    q = q_ref[...]

    @pl.when(c == 0)
    def _():
        m_scr[...] = jnp.full(m_scr.shape, NEG_INF, F32)
        l_scr[...] = jnp.zeros_like(l_scr)
        acc_scr[...] = jnp.zeros_like(acc_scr)

    def attend(k_all, v_all):
        n = k_all.shape[0]
        for h in range(N_HEADS):
            v_h = v_all[:, h * hw:(h + 1) * hw]
            for j in range(2):
                hm = 2 * h + j
                cols = slice(h * hw + j * hd, h * hw + (j + 1) * hd)
                s = _nt_dot(q[:, cols], k_all[:, cols])
                m_old = m_scr[hm]
                m_new = jnp.maximum(m_old, jnp.max(s, axis=1, keepdims=True))
                alpha = jnp.exp(m_old - m_new)
                p = jnp.exp(s - m_new[:, 0:1])
                if n % LANES == 0:
                    psum = p[:, 0:LANES]
                    for g in range(1, n // LANES):
                        psum = psum + p[:, g * LANES:(g + 1) * LANES]
                else:
                    psum = jnp.sum(p, axis=1, keepdims=True) * (1.0 / LANES)
                m_scr[hm] = m_new
                l_scr[hm] = alpha * l_scr[hm] + psum
                acc_scr[hm] = (jnp.concatenate([alpha] * (hw // LANES), axis=1) * acc_scr[hm]
                               + jnp.dot(p.astype(BF16), v_h, preferred_element_type=F32))

    attend(pltpu.einshape("mhd->m(hd)", ck_ref[0, 0]).astype(BF16),
           pltpu.einshape("mhd->m(hd)", cv_ref[0, 0]).astype(BF16))

    @pl.when(c == n_c - 1)
    def _():
        attend(kn_ref[...], vn_ref[...])
        lam = _diff_lambda(lam_ref, lambda_init)
        for h in range(N_HEADS):
            outs = [acc_scr[2 * h + j] / jnp.sum(l_scr[2 * h + j], axis=1, keepdims=True)
                    for j in range(2)]
            o = outs[0] - lam * outs[1]
            o_ref[:, h * hw:(h + 1) * hw] = _head_norm(o, subln_ref, lambda_init).astype(BF16)


def _attn_sample(q, cache_k, cache_v, layer, kb, vb, t, lam_p, subln, lambda_init):
    m, d_att = q.shape
    n_seq = m // t
    past = cache_k.shape[2]
    hw = d_att // N_HEADS
    pc = min(SAMPLE_KEY_TILE, past)
    assert past % pc == 0
    new_blk = pl.BlockSpec((t, d_att), lambda b, c: (b, 0))
    cache_blk = lambda arr: pl.BlockSpec((1, 1, pc) + arr.shape[3:], lambda b, c: (layer, b, c, 0, 0))
    return pl.pallas_call(
        functools.partial(_attn_sample_kernel, lambda_init=lambda_init),
        grid=(n_seq, past // pc),
        in_specs=[
            new_blk, cache_blk(cache_k), cache_blk(cache_v), new_blk, new_blk,
            pl.BlockSpec(lam_p.shape, lambda b, c: (0, 0)),
            pl.BlockSpec((1, hw), lambda b, c: (0, 0)),
        ],
        out_specs=new_blk,
        out_shape=jax.ShapeDtypeStruct((m, d_att), BF16),
        scratch_shapes=[
            pltpu.VMEM((2 * N_HEADS, t, LANES), F32),
            pltpu.VMEM((2 * N_HEADS, t, LANES), F32),
            pltpu.VMEM((2 * N_HEADS, t, hw), F32),
        ],
        compiler_params=_params(2),
        name="attn_sample",
    )(q, cache_k, cache_v, kb, vb, lam_p, subln.reshape(1, hw))


def _proj_residual_kernel(o_ref, w_ref, x_ref, out_ref):
    out_ref[...] = x_ref[...] + jnp.dot(o_ref[...], w_ref[...], preferred_element_type=F32)


def _proj_residual(o, w, x):
    m, kdim = o.shape
    n = w.shape[1]
    tm = min(QKV_ROW_TILE, m)
    return pl.pallas_call(
        _proj_residual_kernel,
        grid=(m // tm,),
        in_specs=[
            pl.BlockSpec((tm, kdim), lambda r: (r, 0)),
            pl.BlockSpec((kdim, n), lambda r: (0, 0), pipeline_mode=pl.Buffered(1)),
            pl.BlockSpec((tm, n), lambda r: (r, 0)),
        ],
        out_specs=pl.BlockSpec((tm, n), lambda r: (r, 0)),
        out_shape=jax.ShapeDtypeStruct((m, n), F32),
        compiler_params=_params(1),
        name="attn_out_proj",
    )(o, w, x)


def kernel(x_prompt, x_sample, state_rglru_conv, state_rglru_h, cache_attn_k, cache_attn_v, state_ffn_conv, rg_norm, rg_w_in, rg_conv_w, rg_conv_b, rg_gate_w, rg_gate_b, rg_log_lambda, rg_w_out, at_norm, at_w_qkv, at_lambda, at_subln, at_w_out, ffn_norm, ffn_w_up, ffn_conv_w, ffn_conv_b, ffn_w_down, final_norm):
    bp, tp, d = x_prompt.shape
    bs, ts, _ = x_sample.shape
    depth = ffn_norm.shape[0]
    c_rnn = rg_w_out.shape[1]
    d_att = at_w_out.shape[1]
    head_dim = d_att // (2 * N_HEADS)

    xp = x_prompt.reshape(bp * tp, d)
    xs = x_sample.reshape(bs * ts, d)
    p_rg_conv, p_rg_h, p_k, p_v, p_ffn = [], [], [], [], []
    s_rg_conv, s_rg_h, s_k, s_v, s_ffn = [], [], [], [], []
    for layer in range(depth):
        j = layer // 2
        if layer % 2 == 0:
            w = (rg_norm[j], rg_w_in[j].astype(BF16), rg_conv_w[j], rg_conv_b[j],
                 rg_gate_w[j].astype(BF16), rg_gate_b[j], rg_log_lambda[j], rg_w_out[j].astype(BF16))
            zero_hist = jnp.zeros((bp, rg_conv_w.shape[1] - 1, c_rnn), F32)
            zero_h = jnp.zeros((bp, 1, c_rnn), F32)
            xp, cp, hp = _rg_layer(xp, zero_hist, zero_h, tp, *w)
            xs, cs, hs = _rg_layer(xs, state_rglru_conv[j], state_rglru_h[j].reshape(bs, 1, c_rnn),
                                   ts, *w)
            p_rg_conv.append(cp); p_rg_h.append(hp.reshape(bp, c_rnn))
            s_rg_conv.append(cs); s_rg_h.append(hs.reshape(bs, c_rnn))
        else:
            lambda_init = 0.8 - 0.6 * math.exp(-0.3 * layer)
            w_qkv = at_w_qkv[j].astype(BF16)
            w_out = at_w_out[j].astype(BF16)
            scale = head_dim ** -0.5
            qp, xnp = _q_proj(xp, at_norm[j], w_qkv, scale)
            kp, kpb = _kv_proj(xnp, w_qkv, 1, 2 * N_HEADS)
            vp, vpb = _kv_proj(xnp, w_qkv, 2, N_HEADS)
            op = _attn_prompt(qp, kpb, vpb, tp, at_lambda[j], at_subln[j], lambda_init)
            xp = _proj_residual(op, w_out, xp)
            qs, xns = _q_proj(xs, at_norm[j], w_qkv, scale)
            kn, knb = _kv_proj(xns, w_qkv, 1, 2 * N_HEADS)
            vn, vnb = _kv_proj(xns, w_qkv, 2, N_HEADS)
            os_ = _attn_sample(qs, cache_attn_k, cache_attn_v, j, knb, vnb, ts,
                               at_lambda[j], at_subln[j], lambda_init)
            xs = _proj_residual(os_, w_out, xs)
            p_k.append(kp.reshape(bp, tp, 2 * N_HEADS, head_dim))
            p_v.append(vp.reshape(bp, tp, N_HEADS, 2 * head_dim))
            s_k.append(kn.reshape(bs, ts, 2 * N_HEADS, head_dim))
            s_v.append(vn.reshape(bs, ts, N_HEADS, 2 * head_dim))
        last = layer == depth - 1
        fw = (ffn_norm[layer], ffn_w_up[layer].astype(BF16), ffn_conv_w[layer], ffn_conv_b[layer],
              ffn_w_down[layer].astype(BF16), final_norm if last else None)
        zero_ffn = jnp.zeros((bp,) + state_ffn_conv.shape[2:], F32)
        xp, fcp = _ffn_layer(xp, zero_ffn, tp, *fw)
        xs, fcs = _ffn_layer(xs, state_ffn_conv[layer], ts, *fw)
        p_ffn.append(fcp); s_ffn.append(fcs)
    stack = lambda arrs: arrs[0][None] if len(arrs) == 1 else jnp.stack(arrs)
    return (xp.reshape(bp, tp, d), xs.reshape(bs, ts, d),
            stack(p_rg_conv), stack(p_rg_h), stack(p_k), stack(p_v), stack(p_ffn),
            stack(s_rg_conv), stack(s_rg_h), stack(s_k), stack(s_v), stack(s_ffn))
```

```python
import functools
import math

import jax
import jax.numpy as jnp
from jax import lax
from jax.experimental import pallas as pl
from jax.experimental.pallas import tpu as pltpu

F32 = jnp.float32
BF16 = jnp.bfloat16

EPS = 1e-6
NEG_INF = -1e30
CHUNK = 64
N_HEADS = 8
RG_BLOCKS = 16
RG_C = 8.0
SUBLANES = 8
LANES = 128
VMEM_LIMIT_BYTES = 58 * 1024 * 1024

LAYER_ROW_TILE = 1024
PART_ROWS = 512
QKV_ROW_TILE = 512
RG_COL_TILE = 256
FF_COL_TILE = 512
ATTN_TILE = 512
SAMPLE_KEY_TILE = 512


def _params(n_axes):
    return pltpu.CompilerParams(
        dimension_semantics=("arbitrary",) * n_axes,
        vmem_limit_bytes=VMEM_LIMIT_BYTES,
    )


def _rmsnorm_rows(x, g):
    ms = jnp.mean(x * x, axis=-1, keepdims=True)
    return (x * lax.rsqrt(ms + EPS)) * g


def _row_tiling(n_seq, t):
    if t >= LAYER_ROW_TILE:
        assert t % LAYER_ROW_TILE == 0
        return LAYER_ROW_TILE, 1, t // LAYER_ROW_TILE
    assert LAYER_ROW_TILE % t == 0 and t % SUBLANES == 0
    nseg = min(LAYER_ROW_TILE // t, n_seq)
    assert n_seq % nseg == 0
    return t, nseg, 1


def _part_rows(tseg, nseg):
    tm = tseg * nseg
    if nseg == 1 and tm % PART_ROWS == 0:
        return PART_ROWS
    return tm


def _last_tile(state, n_seq, tiles_per_seq):
    if tiles_per_seq == 1:
        return state
    return state.reshape((n_seq, tiles_per_seq) + state.shape[1:])[:, -1]


def _x_tile_copy(x_hbm, x_buf, sem, r):
    tm = x_buf.shape[0]
    return pltpu.make_async_copy(x_hbm.at[pl.ds(pl.multiple_of(r * tm, tm), tm)], x_buf, sem)


def _stream_x_tile(x_hbm, x_buf, sem, consume):
    r = pl.program_id(0)
    c = pl.program_id(1)
    n_r = pl.num_programs(0)

    @pl.when(c == 0)
    def _():
        @pl.when(r == 0)
        def _():
            _x_tile_copy(x_hbm, x_buf, sem, r).start()

        _x_tile_copy(x_hbm, x_buf, sem, r).wait()
        consume()

    @pl.when(jnp.logical_and(c == 1, r + 1 < n_r))
    def _():
        _x_tile_copy(x_hbm, x_buf, sem, r + 1).start()


def _rg_kernel(x_hbm, hist_ref, h0_ref, g_ref, wg0_ref, wr0_ref, wgn_ref, wrn_ref, cw_ref, cb_ref,
               gw_ref, gb_ref, ll_ref, wo_ref,
               out_ref, nh_ref, hl_ref,
               x_buf, x_sem, xn_scr, gbuf0, rbuf0, gbuf1, rbuf1, xc_scr, a_scr, b_scr, ccar, hcar,
               *, tseg, nseg, tiles_per_seq, blk, part):
    r = pl.program_id(0)
    c = pl.program_id(1)
    conv_w = cw_ref.shape[0]
    tc = wgn_ref.shape[1]
    tm = tseg * nseg
    tb = r % tiles_per_seq
    pad = SUBLANES - (conv_w - 1)
    n_parts = tm // part
    slices_per_part = 2 + tc // blk if nseg == 1 else 1
    n_slices = n_parts * slices_per_part
    slice_rows = tm // n_slices

    def in_proj(wg_ref, wr_ref, gbuf, rbuf, q):
        rows = slice(q * slice_rows, (q + 1) * slice_rows)
        xn = xn_scr[rows, :]
        gbuf[rows, :] = jnp.dot(xn, wg_ref[...], preferred_element_type=F32)
        rec = jnp.dot(xn, wr_ref[...], preferred_element_type=F32)
        if nseg == 1:
            rbuf[0, SUBLANES + q * slice_rows:SUBLANES + (q + 1) * slice_rows, :] = rec
        else:
            for s in range(q * slice_rows // tseg, (q + 1) * slice_rows // tseg):
                lo = s * tseg - q * slice_rows
                rbuf[s, SUBLANES:SUBLANES + tseg, :] = rec[lo:lo + tseg]

    def consume():
        x = x_buf[...]
        xn_scr[...] = _rmsnorm_rows(x, g_ref[...]).astype(BF16)
        out_ref[...] = x
        for q in range(n_slices):
            in_proj(wg0_ref, wr0_ref, gbuf0, rbuf0, q)

    _stream_x_tile(x_hbm, x_buf, x_sem, consume)

    def step(gbuf, rbuf, gbuf_next, rbuf_next):
        for s in range(nseg):
            if tiles_per_seq == 1:
                rbuf[s, pad:SUBLANES, :] = hist_ref[s]
            else:
                @pl.when(tb == 0)
                def _():
                    rbuf[s, pad:SUBLANES, :] = hist_ref[s]
                    hcar[c] = h0_ref[s]

                @pl.when(tb != 0)
                def _():
                    rbuf[s, pad:SUBLANES, :] = ccar[c]

        def next_slice(p, k):
            if k < slices_per_part:
                in_proj(wgn_ref, wrn_ref, gbuf_next, rbuf_next, slices_per_part * p + k)

        z = -ll_ref[...]
        softplus = jnp.maximum(z, 0.0) + jnp.log1p(jnp.exp(-jnp.abs(z)))

        h_run = None
        for p in range(n_parts):
            rows = slice(p * part, (p + 1) * part)

            segs = [(0, p * part, part)] if nseg == 1 else [(s, 0, tseg) for s in range(nseg)]
            for s, start, length in segs:
                xc = cb_ref[...] + rbuf[s, pad + start:pad + start + length, :] * cw_ref[0:1, :]
                for k in range(1, conv_w):
                    lo = pad + start + k
                    xc = xc + rbuf[s, lo:lo + length, :] * cw_ref[k:k + 1, :]
                lo = start if nseg == 1 else s * tseg
                xc_scr[lo:lo + length, :] = xc

            next_slice(p, 0)

            for n in range(tc // blk):
                cols = slice(n * blk, (n + 1) * blk)
                xc_n = xc_scr[rows, cols]
                gate = jnp.dot(xc_n.astype(BF16), gw_ref[n], preferred_element_type=F32) + gb_ref[n]
                gate = jax.nn.sigmoid(gate)
                log_a = (-RG_C * gate[:, :blk]) * softplus[:, cols]
                a_n = jnp.exp(log_a)
                a_scr[rows, cols] = a_n
                var = 1.0 - a_n * a_n
                mult = jnp.where(var > 0.0, var * lax.rsqrt(var), 0.0)
                b_scr[rows, cols] = mult * gate[:, blk:] * xc_n
                next_slice(p, 1 + n)

            a = a_scr[rows, :].reshape(part // SUBLANES, SUBLANES, tc)
            b = b_scr[rows, :].reshape(part // SUBLANES, SUBLANES, tc)
            row = lax.broadcasted_iota(jnp.int32, a.shape, 1)
            shift = 1
            while shift < SUBLANES:
                keep = row >= shift
                a_prev = pltpu.roll(a, shift, axis=1)
                b_prev = pltpu.roll(b, shift, axis=1)
                b = jnp.where(keep, a * b_prev + b, b)
                a = jnp.where(keep, a * a_prev, a)
                shift *= 2
            a_scr[rows, :] = a.reshape(part, tc)
            b_scr[rows, :] = b.reshape(part, tc)

            next_slice(p, 1 + tc // blk)

            for s, start, length in segs:
                lo = start if nseg == 1 else s * tseg
                if nseg > 1 or p == 0:
                    h_run = h0_ref[s] if tiles_per_seq == 1 else hcar[c]

                for i in range(length // SUBLANES):
                    off = lo + i * SUBLANES
                    h = a_scr[off:off + SUBLANES, :] * h_run + b_scr[off:off + SUBLANES, :]
                    b_scr[off:off + SUBLANES, :] = h
                    h_run = h[SUBLANES - 1:SUBLANES, :]
                if nseg > 1 or p == n_parts - 1:
                    hl_ref[s] = h_run
                    if tiles_per_seq > 1:
                        hcar[c] = h_run

            gated = b_scr[rows, :] * jax.nn.gelu(gbuf[rows, :])
            out_ref[rows, :] += jnp.dot(gated.astype(BF16), wo_ref[...], preferred_element_type=F32)

        for s in range(nseg):
            new_hist = rbuf[s, pad + tseg:SUBLANES + tseg, :]
            nh_ref[s] = new_hist
            if tiles_per_seq > 1:
                ccar[c] = new_hist

    @pl.when(c % 2 == 0)
    def _():
        step(gbuf0, rbuf0, gbuf1, rbuf1)

    @pl.when(c % 2 == 1)
    def _():
        step(gbuf1, rbuf1, gbuf0, rbuf0)


def _rg_layer(x, hist, h0, t, norm_g, w_in, conv_w, conv_b, gate_w, gate_b, log_lam, w_out):
    m, d = x.shape
    n_seq = m // t
    c_rnn = w_out.shape[0]
    conv_width = conv_w.shape[0]
    blk = c_rnn // RG_BLOCKS
    tc = min(RG_COL_TILE, c_rnn)
    n_c = c_rnn // tc
    assert n_c >= 2 and n_c % 2 == 0
    tseg, nseg, tps = _row_tiling(n_seq, t)
    tm = tseg * nseg
    nxt = lambda c: jnp.minimum(c + 1, n_c - 1)
    kern = functools.partial(_rg_kernel, tseg=tseg, nseg=nseg, tiles_per_seq=tps, blk=blk,
                             part=_part_rows(tseg, nseg))
    seq_blk = lambda r, c: (r // tps, 0, c)
    out, new_hist, h_last = pl.pallas_call(
        kern,
        grid=(m // tm, n_c),
        in_specs=[
            pl.BlockSpec(memory_space=pl.ANY),
            pl.BlockSpec((nseg, conv_width - 1, tc), seq_blk),
            pl.BlockSpec((nseg, 1, tc), seq_blk),
            pl.BlockSpec((1, d), lambda r, c: (0, 0)),
            pl.BlockSpec((d, tc), lambda r, c: (0, 0), pipeline_mode=pl.Buffered(1)),
            pl.BlockSpec((d, tc), lambda r, c: (0, n_c), pipeline_mode=pl.Buffered(1)),
            pl.BlockSpec((d, tc), lambda r, c: (0, nxt(c))),
            pl.BlockSpec((d, tc), lambda r, c: (0, n_c + nxt(c))),
            pl.BlockSpec((conv_width, tc), lambda r, c: (0, c)),
            pl.BlockSpec((1, tc), lambda r, c: (0, c)),
            pl.BlockSpec((tc // blk, blk, 2 * blk), lambda r, c: (c, 0, 0)),
            pl.BlockSpec((tc // blk, 1, 2 * blk), lambda r, c: (c, 0, 0)),
            pl.BlockSpec((1, tc), lambda r, c: (0, c)),
            pl.BlockSpec((tc, d), lambda r, c: (c, 0)),
        ],
        out_specs=[
            pl.BlockSpec((tm, d), lambda r, c: (r, 0)),
            pl.BlockSpec((nseg, conv_width - 1, tc), lambda r, c: (r, 0, c)),
            pl.BlockSpec((nseg, 1, tc), lambda r, c: (r, 0, c)),
        ],
        out_shape=[
            jax.ShapeDtypeStruct((m, d), F32),
            jax.ShapeDtypeStruct((n_seq * tps, conv_width - 1, c_rnn), F32),
            jax.ShapeDtypeStruct((n_seq * tps, 1, c_rnn), F32),
        ],
        scratch_shapes=[
            pltpu.VMEM((tm, d), F32),
            pltpu.SemaphoreType.DMA,
            pltpu.VMEM((tm, d), BF16),
            pltpu.VMEM((tm, tc), F32),
            pltpu.VMEM((nseg, SUBLANES + tseg, tc), F32),
            pltpu.VMEM((tm, tc), F32),
            pltpu.VMEM((nseg, SUBLANES + tseg, tc), F32),
            pltpu.VMEM((tm, tc), F32),
            pltpu.VMEM((tm, tc), F32),
            pltpu.VMEM((tm, tc), F32),
            pltpu.VMEM((n_c, conv_width - 1, tc), F32),
            pltpu.VMEM((n_c, 1, tc), F32),
        ],
        compiler_params=_params(2),
        name="rg_layer",
    )(x, hist, h0, norm_g.reshape(1, d), w_in, w_in, w_in, w_in, conv_w, conv_b.reshape(1, c_rnn),
      gate_w, gate_b.reshape(RG_BLOCKS, 1, 2 * blk), log_lam.reshape(1, c_rnn), w_out)
    return out, _last_tile(new_hist, n_seq, tps), _last_tile(h_last, n_seq, tps)


def _ffn_kernel(x_hbm, hist_ref, g_ref, wg_ref, wv_ref, cwg_ref, cwv_ref, cbg_ref, cbv_ref, wd_ref,
                fg_ref, out_ref, nh_ref,
                x_buf, x_sem, xn_scr, ubuf, hid_scr, ccar,
                *, tseg, nseg, tiles_per_seq, final_norm, part):
    r = pl.program_id(0)
    c = pl.program_id(1)
    n_c = pl.num_programs(1)
    conv_w = cwg_ref.shape[0]
    tm = tseg * nseg
    tb = r % tiles_per_seq
    pad = SUBLANES - (conv_w - 1)
    halves = ((wg_ref, cwg_ref, cbg_ref), (wv_ref, cwv_ref, cbv_ref))

    def consume():
        x = x_buf[...]
        xn_scr[...] = _rmsnorm_rows(x, g_ref[...]).astype(BF16)
        out_ref[...] = x

    _stream_x_tile(x_hbm, x_buf, x_sem, consume)

    for s in range(nseg):
        for half in range(2):
            def load_state(half=half, s=s):
                for k in range(conv_w - 1):
                    ubuf[half, s, pad + k:pad + k + 1, :] = hist_ref[s, k, half:half + 1, :]

            if tiles_per_seq == 1:
                load_state()
            else:
                pl.when(tb == 0)(load_state)

                @pl.when(tb != 0)
                def _():
                    ubuf[half, s, pad:SUBLANES, :] = ccar[c, half]

    n_parts = tm // part
    for p in range(n_parts):
        xn = xn_scr[p * part:(p + 1) * part, :]
        for half, (w_ref, _, _) in enumerate(halves):
            u = jnp.dot(xn, w_ref[...], preferred_element_type=F32)
            if nseg == 1:
                ubuf[half, 0, SUBLANES + p * part:SUBLANES + (p + 1) * part, :] = u
            else:
                for s in range(nseg):
                    ubuf[half, s, SUBLANES:SUBLANES + tseg, :] = u[s * tseg:(s + 1) * tseg]

    for p in range(n_parts):
        rows = slice(p * part, (p + 1) * part)
        segs = [(0, p * part, part)] if nseg == 1 else [(s, 0, tseg) for s in range(nseg)]
        for s, start, length in segs:
            conv = []
            for half, (_, cw_ref, cb_ref) in enumerate(halves):
                cv = cb_ref[...] + ubuf[half, s, pad + start:pad + start + length, :] * cw_ref[0:1, :]
                for k in range(1, conv_w):
                    lo = pad + start + k
                    cv = cv + ubuf[half, s, lo:lo + length, :] * cw_ref[k:k + 1, :]
                conv.append(cv)
            hid = (jax.nn.gelu(conv[0]) * conv[1]).astype(BF16)
            if nseg > 1:
                hid_scr[s * tseg:(s + 1) * tseg, :] = hid
        if nseg > 1:
            hid = hid_scr[...]
        out_ref[rows, :] += jnp.dot(hid, wd_ref[...], preferred_element_type=F32)

    for s in range(nseg):
        for half in range(2):
            for k in range(conv_w - 1):
                nh_ref[s, k, half:half + 1, :] = ubuf[half, s, pad + tseg + k:pad + tseg + k + 1, :]
            if tiles_per_seq > 1:
                ccar[c, half] = ubuf[half, s, pad + tseg:SUBLANES + tseg, :]

    if final_norm:
        @pl.when(c == n_c - 1)
        def _():
            out_ref[...] = _rmsnorm_rows(out_ref[...], fg_ref[...])


def _ffn_layer(x, hist, t, norm_g, w_up, conv_w, conv_b, w_down, final_g):
    m, d = x.shape
    n_seq = m // t
    d_ff = w_down.shape[0]
    conv_width = conv_w.shape[0]
    tf = FF_COL_TILE
    assert d_ff % tf == 0
    n_c = d_ff // tf
    assert n_c >= 2
    tseg, nseg, tps = _row_tiling(n_seq, t)
    tm = tseg * nseg
    final_norm = final_g is not None
    if final_g is None:
        final_g = jnp.ones((d,), F32)
    kern = functools.partial(_ffn_kernel, tseg=tseg, nseg=nseg, tiles_per_seq=tps,
                             final_norm=final_norm, part=_part_rows(tseg, nseg))
    hist4 = hist.reshape(n_seq, conv_width - 1, 2, d_ff)
    conv_b2 = conv_b.reshape(1, 2 * d_ff)
    out, new_hist = pl.pallas_call(
        kern,
        grid=(m // tm, n_c),
        in_specs=[
            pl.BlockSpec(memory_space=pl.ANY),
            pl.BlockSpec((nseg, conv_width - 1, 2, tf), lambda r, c: (r // tps, 0, 0, c)),
            pl.BlockSpec((1, d), lambda r, c: (0, 0)),
            pl.BlockSpec((d, tf), lambda r, c: (0, c)),
            pl.BlockSpec((d, tf), lambda r, c: (0, n_c + c)),
            pl.BlockSpec((conv_width, tf), lambda r, c: (0, c)),
            pl.BlockSpec((conv_width, tf), lambda r, c: (0, n_c + c)),
            pl.BlockSpec((1, tf), lambda r, c: (0, c)),
            pl.BlockSpec((1, tf), lambda r, c: (0, n_c + c)),
            pl.BlockSpec((tf, d), lambda r, c: (c, 0)),
            pl.BlockSpec((1, d), lambda r, c: (0, 0)),
        ],
        out_specs=[
            pl.BlockSpec((tm, d), lambda r, c: (r, 0)),
            pl.BlockSpec((nseg, conv_width - 1, 2, tf), lambda r, c: (r, 0, 0, c)),
        ],
        out_shape=[
            jax.ShapeDtypeStruct((m, d), F32),
            jax.ShapeDtypeStruct((n_seq * tps, conv_width - 1, 2, d_ff), F32),
        ],
        scratch_shapes=[
            pltpu.VMEM((tm, d), F32),
            pltpu.SemaphoreType.DMA,
            pltpu.VMEM((tm, d), BF16),
            pltpu.VMEM((2, nseg, SUBLANES + tseg, tf), F32),
            pltpu.VMEM((tm if nseg > 1 else SUBLANES, tf), BF16),
            pltpu.VMEM((n_c, 2, conv_width - 1, tf), F32),
        ],
        compiler_params=_params(2),
        name="conv_ffn",
    )(x, hist4, norm_g.reshape(1, d), w_up, w_up, conv_w, conv_w, conv_b2, conv_b2, w_down,
      final_g.reshape(1, d))
    return out, _last_tile(new_hist, n_seq, tps).reshape(n_seq, conv_width - 1, 2 * d_ff)


def _q_kernel(x_ref, g_ref, w_ref, q_ref, xn_ref, *, scale):
    xn = _rmsnorm_rows(x_ref[...], g_ref[...]).astype(BF16)
    xn_ref[...] = xn
    q_ref[...] = (jnp.dot(xn, w_ref[...], preferred_element_type=F32) * scale).astype(BF16)


def _q_proj(x, norm_g, w_qkv, scale):
    m, d = x.shape
    d_att = w_qkv.shape[1] // 3
    tm = min(QKV_ROW_TILE, m)
    return pl.pallas_call(
        functools.partial(_q_kernel, scale=scale),
        grid=(m // tm,),
        in_specs=[
            pl.BlockSpec((tm, d), lambda r: (r, 0)),
            pl.BlockSpec((1, d), lambda r: (0, 0)),
            pl.BlockSpec((d, d_att), lambda r: (0, 0), pipeline_mode=pl.Buffered(1)),
        ],
        out_specs=[
            pl.BlockSpec((tm, d_att), lambda r: (r, 0)),
            pl.BlockSpec((tm, d), lambda r: (r, 0)),
        ],
        out_shape=[
            jax.ShapeDtypeStruct((m, d_att), BF16),
            jax.ShapeDtypeStruct((m, d), BF16),
        ],
        compiler_params=_params(1),
        name="q_proj",
    )(x, norm_g.reshape(1, d), w_qkv)


def _kv_kernel(xn_ref, w_ref, heads_ref, flat_ref):
    y = jnp.dot(xn_ref[...], w_ref[...], preferred_element_type=F32)
    flat_ref[...] = y.astype(BF16)
    heads_ref[...] = pltpu.einshape("m(hd)->mhd", y, h=heads_ref.shape[1])


def _kv_proj(xn, w_qkv, which, n_split):
    m, d = xn.shape
    d_att = w_qkv.shape[1] // 3
    tm = min(QKV_ROW_TILE, m)
    return pl.pallas_call(
        _kv_kernel,
        grid=(m // tm,),
        in_specs=[
            pl.BlockSpec((tm, d), lambda r: (r, 0)),
            pl.BlockSpec((d, d_att), lambda r: (0, which), pipeline_mode=pl.Buffered(1)),
        ],
        out_specs=[
            pl.BlockSpec((tm, n_split, d_att // n_split), lambda r: (r, 0, 0)),
            pl.BlockSpec((tm, d_att), lambda r: (r, 0)),
        ],
        out_shape=[
            jax.ShapeDtypeStruct((m, n_split, d_att // n_split), F32),
            jax.ShapeDtypeStruct((m, d_att), BF16),
        ],
        compiler_params=_params(1),
        name="kv_proj",
    )(xn, w_qkv)


def _diff_lambda(lam_ref, lambda_init):
    lam = lam_ref[...]
    s1 = jnp.sum(lam[0:1] * lam[1:2], axis=-1, keepdims=True)
    s2 = jnp.sum(lam[2:3] * lam[3:4], axis=-1, keepdims=True)
    return jnp.exp(s1) - jnp.exp(s2) + lambda_init


def _head_norm(o, subln_ref, lambda_init):
    return _rmsnorm_rows(o, subln_ref[...]) * (1.0 - lambda_init)


def _nt_dot(a, b):
    return lax.dot_general(a, b, (((1,), (1,)), ((), ())), preferred_element_type=F32)


def _attn_prompt_kernel(q_ref, k_ref, v_ref, lam_ref, subln_ref, o_ref, *, lambda_init, tq):
    t, hw = q_ref.shape
    hd = hw // 2
    lam = _diff_lambda(lam_ref, lambda_init)

    def fold(x, op):
        acc = x[:, 0:LANES]
        for g in range(1, x.shape[1] // LANES):
            acc = op(acc, x[:, g * LANES:(g + 1) * LANES])
        return acc

    def spread(x, width):
        return jnp.concatenate([x] * (width // LANES), axis=1)

    def row_max(s):
        return jnp.broadcast_to(jnp.max(fold(s, jnp.maximum), axis=1, keepdims=True),
                                (s.shape[0], LANES))

    q_chunk = lax.broadcasted_iota(jnp.int32, (tq, tq), 0) // CHUNK
    k_chunk = lax.broadcasted_iota(jnp.int32, (tq, tq), 1) // CHUNK
    visible = k_chunk <= q_chunk
    for i in range(t // tq):
        rows = slice(i * tq, (i + 1) * tq)
        outs = []
        for j in range(2):
            cols = slice(j * hd, (j + 1) * hd)
            q = q_ref[rows, cols]
            s = jnp.where(visible, _nt_dot(q, k_ref[rows, cols]), NEG_INF)
            m = row_max(s)
            p = jnp.exp2(s - spread(m, tq))
            l = fold(p, jnp.add)
            acc = jnp.dot(p.astype(BF16), v_ref[rows, :], preferred_element_type=F32)
            for kb in range(i):
                keys = slice(kb * tq, (kb + 1) * tq)
                s = _nt_dot(q, k_ref[keys, cols])
                m_new = jnp.maximum(m, row_max(s))
                alpha = jnp.exp2(m - m_new)
                p = jnp.exp2(s - spread(m_new, tq))
                l = alpha * l + fold(p, jnp.add)
                acc = spread(alpha, hw) * acc + jnp.dot(p.astype(BF16), v_ref[keys, :],
                                                        preferred_element_type=F32)
                m = m_new
            outs.append(acc / jnp.sum(l, axis=1, keepdims=True))
        o = outs[0] - lam * outs[1]
        o_ref[rows, :] = _head_norm(o, subln_ref, lambda_init).astype(BF16)


def _attn_prompt(q, kb, vb, t, lam_p, subln, lambda_init):
    m, d_att = q.shape
    n_seq = m // t
    hw = d_att // N_HEADS
    tq = min(ATTN_TILE, t)
    assert t % tq == 0 and tq % CHUNK == 0
    seq_head = pl.BlockSpec((t, hw), lambda b, h: (b, h))
    return pl.pallas_call(
        functools.partial(_attn_prompt_kernel, lambda_init=lambda_init, tq=tq),
        grid=(n_seq, N_HEADS),
        in_specs=[
            seq_head, seq_head, seq_head,
            pl.BlockSpec(lam_p.shape, lambda b, h: (0, 0)),
            pl.BlockSpec((1, hw), lambda b, h: (0, 0)),
        ],
        out_specs=seq_head,
        out_shape=jax.ShapeDtypeStruct((m, d_att), BF16),
        compiler_params=_params(2),
        name="attn_prompt",
    )(q, kb, vb, lam_p, subln.reshape(1, hw))


def _attn_sample_kernel(q_ref, ck_ref, cv_ref, kn_ref, vn_ref, lam_ref, subln_ref, o_ref,
                        m_scr, l_scr, acc_scr, *, lambda_init):
    c = pl.program_id(1)
    n_c = pl.num_programs(1)
    t, d_att = q_ref.shape
    hw = d_att // N_HEADS
    hd = hw // 2
    q = q_ref[...]

    @pl.when(c == 0)
    def _():
        m_scr[...] = jnp.full(m_scr.shape, NEG_INF, F32)
        l_scr[...] = jnp.zeros_like(l_scr)
        acc_scr[...] = jnp.zeros_like(acc_scr)

    def attend(k_all, v_all):
        n = k_all.shape[0]
        for h in range(N_HEADS):
            v_h = v_all[:, h * hw:(h + 1) * hw]
            for j in range(2):
                hm = 2 * h + j
                cols = slice(h * hw + j * hd, h * hw + (j + 1) * hd)
                s = _nt_dot(q[:, cols], k_all[:, cols])
                m_old = m_scr[hm]
                m_new = jnp.maximum(m_old, jnp.max(s, axis=1, keepdims=True))
                alpha = jnp.exp2(m_old - m_new)
                p = jnp.exp2(s - m_new[:, 0:1])
                if n % LANES == 0:
                    psum = p[:, 0:LANES]
                    for g in range(1, n // LANES):
                        psum = psum + p[:, g * LANES:(g + 1) * LANES]
                else:
                    psum = jnp.sum(p, axis=1, keepdims=True) * (1.0 / LANES)
                m_scr[hm] = m_new
                l_scr[hm] = alpha * l_scr[hm] + psum
                acc_scr[hm] = (jnp.concatenate([alpha] * (hw // LANES), axis=1) * acc_scr[hm]
                               + jnp.dot(p.astype(BF16), v_h, preferred_element_type=F32))

    attend(pltpu.einshape("mhd->m(hd)", ck_ref[0, 0]).astype(BF16),
           pltpu.einshape("mhd->m(hd)", cv_ref[0, 0]).astype(BF16))

    @pl.when(c == n_c - 1)
    def _():
        attend(kn_ref[...], vn_ref[...])
        lam = _diff_lambda(lam_ref, lambda_init)
        for h in range(N_HEADS):
            outs = [acc_scr[2 * h + j] / jnp.sum(l_scr[2 * h + j], axis=1, keepdims=True)
                    for j in range(2)]
            o = outs[0] - lam * outs[1]
            o_ref[:, h * hw:(h + 1) * hw] = _head_norm(o, subln_ref, lambda_init).astype(BF16)


def _attn_sample(q, cache_k, cache_v, layer, kb, vb, t, lam_p, subln, lambda_init):
    m, d_att = q.shape
    n_seq = m // t
    past = cache_k.shape[2]
    hw = d_att // N_HEADS
    pc = min(SAMPLE_KEY_TILE, past)
    assert past % pc == 0
    new_blk = pl.BlockSpec((t, d_att), lambda b, c: (b, 0))
    cache_blk = lambda arr: pl.BlockSpec((1, 1, pc) + arr.shape[3:], lambda b, c: (layer, b, c, 0, 0))
    return pl.pallas_call(
        functools.partial(_attn_sample_kernel, lambda_init=lambda_init),
        grid=(n_seq, past // pc),
        in_specs=[
            new_blk, cache_blk(cache_k), cache_blk(cache_v), new_blk, new_blk,
            pl.BlockSpec(lam_p.shape, lambda b, c: (0, 0)),
            pl.BlockSpec((1, hw), lambda b, c: (0, 0)),
        ],
        out_specs=new_blk,
        out_shape=jax.ShapeDtypeStruct((m, d_att), BF16),
        scratch_shapes=[
            pltpu.VMEM((2 * N_HEADS, t, LANES), F32),
            pltpu.VMEM((2 * N_HEADS, t, LANES), F32),
            pltpu.VMEM((2 * N_HEADS, t, hw), F32),
        ],
        compiler_params=_params(2),
        name="attn_sample",
    )(q, cache_k, cache_v, kb, vb, lam_p, subln.reshape(1, hw))


def _proj_residual_kernel(o_ref, w_ref, x_ref, out_ref):
    out_ref[...] = x_ref[...] + jnp.dot(o_ref[...], w_ref[...], preferred_element_type=F32)


def _proj_residual(o, w, x):
    m, kdim = o.shape
    n = w.shape[1]
    tm = min(QKV_ROW_TILE, m)
    return pl.pallas_call(
        _proj_residual_kernel,
        grid=(m // tm,),
        in_specs=[
            pl.BlockSpec((tm, kdim), lambda r: (r, 0)),
            pl.BlockSpec((kdim, n), lambda r: (0, 0), pipeline_mode=pl.Buffered(1)),
            pl.BlockSpec((tm, n), lambda r: (r, 0)),
        ],
        out_specs=pl.BlockSpec((tm, n), lambda r: (r, 0)),
        out_shape=jax.ShapeDtypeStruct((m, n), F32),
        compiler_params=_params(1),
        name="attn_out_proj",
    )(o, w, x)


def kernel(x_prompt, x_sample, state_rglru_conv, state_rglru_h, cache_attn_k, cache_attn_v, state_ffn_conv, rg_norm, rg_w_in, rg_conv_w, rg_conv_b, rg_gate_w, rg_gate_b, rg_log_lambda, rg_w_out, at_norm, at_w_qkv, at_lambda, at_subln, at_w_out, ffn_norm, ffn_w_up, ffn_conv_w, ffn_conv_b, ffn_w_down, final_norm):
    bp, tp, d = x_prompt.shape
    bs, ts, _ = x_sample.shape
    depth = ffn_norm.shape[0]
    c_rnn = rg_w_out.shape[1]
    d_att = at_w_out.shape[1]
    head_dim = d_att // (2 * N_HEADS)

    xp = x_prompt.reshape(bp * tp, d)
    xs = x_sample.reshape(bs * ts, d)
    p_rg_conv, p_rg_h, p_k, p_v, p_ffn = [], [], [], [], []
    s_rg_conv, s_rg_h, s_k, s_v, s_ffn = [], [], [], [], []
    for layer in range(depth):
        j = layer // 2
        if layer % 2 == 0:
            w = (rg_norm[j], rg_w_in[j].astype(BF16), rg_conv_w[j], rg_conv_b[j],
                 rg_gate_w[j].astype(BF16), rg_gate_b[j], rg_log_lambda[j], rg_w_out[j].astype(BF16))
            zero_hist = jnp.zeros((bp, rg_conv_w.shape[1] - 1, c_rnn), F32)
            zero_h = jnp.zeros((bp, 1, c_rnn), F32)
            xp, cp, hp = _rg_layer(xp, zero_hist, zero_h, tp, *w)
            xs, cs, hs = _rg_layer(xs, state_rglru_conv[j], state_rglru_h[j].reshape(bs, 1, c_rnn),
                                   ts, *w)
            p_rg_conv.append(cp); p_rg_h.append(hp.reshape(bp, c_rnn))
            s_rg_conv.append(cs); s_rg_h.append(hs.reshape(bs, c_rnn))
        else:
            lambda_init = 0.8 - 0.6 * math.exp(-0.3 * layer)
            w_qkv = at_w_qkv[j].astype(BF16)
            w_out = at_w_out[j].astype(BF16)
            scale = head_dim ** -0.5 * math.log2(math.e)
            qp, xnp = _q_proj(xp, at_norm[j], w_qkv, scale)
            kp, kpb = _kv_proj(xnp, w_qkv, 1, 2 * N_HEADS)
            vp, vpb = _kv_proj(xnp, w_qkv, 2, N_HEADS)
            op = _attn_prompt(qp, kpb, vpb, tp, at_lambda[j], at_subln[j], lambda_init)
            xp = _proj_residual(op, w_out, xp)
            qs, xns = _q_proj(xs, at_norm[j], w_qkv, scale)
            kn, knb = _kv_proj(xns, w_qkv, 1, 2 * N_HEADS)
            vn, vnb = _kv_proj(xns, w_qkv, 2, N_HEADS)
            os_ = _attn_sample(qs, cache_attn_k, cache_attn_v, j, knb, vnb, ts,
                               at_lambda[j], at_subln[j], lambda_init)
            xs = _proj_residual(os_, w_out, xs)
            p_k.append(kp.reshape(bp, tp, 2 * N_HEADS, head_dim))
            p_v.append(vp.reshape(bp, tp, N_HEADS, 2 * head_dim))
            s_k.append(kn.reshape(bs, ts, 2 * N_HEADS, head_dim))
            s_v.append(vn.reshape(bs, ts, N_HEADS, 2 * head_dim))
        last = layer == depth - 1
        fw = (ffn_norm[layer], ffn_w_up[layer].astype(BF16), ffn_conv_w[layer], ffn_conv_b[layer],
              ffn_w_down[layer].astype(BF16), final_norm if last else None)
        zero_ffn = jnp.zeros((bp,) + state_ffn_conv.shape[2:], F32)
        xp, fcp = _ffn_layer(xp, zero_ffn, tp, *fw)
        xs, fcs = _ffn_layer(xs, state_ffn_conv[layer], ts, *fw)
        p_ffn.append(fcp); s_ffn.append(fcs)
    stack = lambda arrs: arrs[0][None] if len(arrs) == 1 else jnp.stack(arrs)
    return (xp.reshape(bp, tp, d), xs.reshape(bs, ts, d),
            stack(p_rg_conv), stack(p_rg_h), stack(p_k), stack(p_v), stack(p_ffn),
            stack(s_rg_conv), stack(s_rg_h), stack(s_k), stack(s_v), stack(s_ffn))
```

```python
import functools
import math

import jax
import jax.numpy as jnp
from jax import lax
from jax.experimental import pallas as pl
from jax.experimental.pallas import tpu as pltpu

F32 = jnp.float32
BF16 = jnp.bfloat16

EPS = 1e-6
NEG_INF = -1e30
CHUNK = 64
N_HEADS = 8
RG_BLOCKS = 16
RG_C = 8.0
SUBLANES = 8
LANES = 128
VMEM_LIMIT_BYTES = 58 * 1024 * 1024

LAYER_ROW_TILE = 1024
PART_ROWS = 512
QKV_ROW_TILE = 512
RG_COL_TILE = 512
FF_COL_TILE = 512
ATTN_TILE = 512
SAMPLE_KEY_TILE = 512


def _params(n_axes):
    return pltpu.CompilerParams(
        dimension_semantics=("arbitrary",) * n_axes,
        vmem_limit_bytes=VMEM_LIMIT_BYTES,
    )


def _rmsnorm_rows(x, g):
    ms = jnp.mean(x * x, axis=-1, keepdims=True)
    return (x * lax.rsqrt(ms + EPS)) * g


def _row_tiling(n_seq, t):
    if t >= LAYER_ROW_TILE:
        assert t % LAYER_ROW_TILE == 0
        return LAYER_ROW_TILE, 1, t // LAYER_ROW_TILE
    assert LAYER_ROW_TILE % t == 0 and t % SUBLANES == 0
    nseg = min(LAYER_ROW_TILE // t, n_seq)
    assert n_seq % nseg == 0
    return t, nseg, 1


def _part_rows(tseg, nseg):
    tm = tseg * nseg
    if nseg == 1 and tm % PART_ROWS == 0:
        return PART_ROWS
    return tm


def _last_tile(state, n_seq, tiles_per_seq):
    if tiles_per_seq == 1:
        return state
    return state.reshape((n_seq, tiles_per_seq) + state.shape[1:])[:, -1]


def _x_tile_copy(x_hbm, x_buf, sem, r):
    tm = x_buf.shape[0]
    return pltpu.make_async_copy(x_hbm.at[pl.ds(pl.multiple_of(r * tm, tm), tm)], x_buf, sem)


def _stream_x_tile(x_hbm, x_buf, sem, consume):
    r = pl.program_id(0)
    c = pl.program_id(1)
    n_r = pl.num_programs(0)

    @pl.when(c == 0)
    def _():
        @pl.when(r == 0)
        def _():
            _x_tile_copy(x_hbm, x_buf, sem, r).start()

        _x_tile_copy(x_hbm, x_buf, sem, r).wait()
        consume()

    @pl.when(jnp.logical_and(c == 1, r + 1 < n_r))
    def _():
        _x_tile_copy(x_hbm, x_buf, sem, r + 1).start()


def _rg_kernel(x_hbm, hist_ref, h0_ref, g_ref, wg_ref, wr_ref, cw_ref, cb_ref, gw_ref, gb_ref,
               ll_ref, wo_ref,
               out_ref, nh_ref, hl_ref,
               x_buf, x_sem, xn_scr, rbuf, xc_scr, a_scr, b_scr, ccar, hcar,
               *, tseg, nseg, tiles_per_seq, blk, part):
    r = pl.program_id(0)
    c = pl.program_id(1)
    conv_w = cw_ref.shape[0]
    tc = wg_ref.shape[1]
    tm = tseg * nseg
    tb = r % tiles_per_seq
    pad = SUBLANES - (conv_w - 1)

    def consume():
        x = x_buf[...]
        xn_scr[...] = _rmsnorm_rows(x, g_ref[...]).astype(BF16)
        out_ref[...] = x

    _stream_x_tile(x_hbm, x_buf, x_sem, consume)

    for s in range(nseg):
        if tiles_per_seq == 1:
            rbuf[s, pad:SUBLANES, :] = hist_ref[s]
        else:
            @pl.when(tb == 0)
            def _():
                rbuf[s, pad:SUBLANES, :] = hist_ref[s]
                hcar[c] = h0_ref[s]

            @pl.when(tb != 0)
            def _():
                rbuf[s, pad:SUBLANES, :] = ccar[c]

    z = -ll_ref[...]
    softplus = jnp.maximum(z, 0.0) + jnp.log1p(jnp.exp(-jnp.abs(z)))

    n_parts = tm // part
    gelu_gate = []
    for p in range(n_parts):
        xn = xn_scr[p * part:(p + 1) * part, :]
        gelu_gate.append(jax.nn.gelu(jnp.dot(xn, wg_ref[...], preferred_element_type=F32)))
        rec = jnp.dot(xn, wr_ref[...], preferred_element_type=F32)
        if nseg == 1:
            rbuf[0, SUBLANES + p * part:SUBLANES + (p + 1) * part, :] = rec
        else:
            for s in range(nseg):
                rbuf[s, SUBLANES:SUBLANES + tseg, :] = rec[s * tseg:(s + 1) * tseg]

    h_run = None
    for p in range(n_parts):
        rows = slice(p * part, (p + 1) * part)

        segs = [(0, p * part, part)] if nseg == 1 else [(s, 0, tseg) for s in range(nseg)]
        for s, start, length in segs:
            xc = cb_ref[...] + rbuf[s, pad + start:pad + start + length, :] * cw_ref[0:1, :]
            for k in range(1, conv_w):
                lo = pad + start + k
                xc = xc + rbuf[s, lo:lo + length, :] * cw_ref[k:k + 1, :]
            lo = start if nseg == 1 else s * tseg
            xc_scr[lo:lo + length, :] = xc

        for n in range(tc // blk):
            cols = slice(n * blk, (n + 1) * blk)
            xc_n = xc_scr[rows, cols]
            gate = jnp.dot(xc_n.astype(BF16), gw_ref[n], preferred_element_type=F32) + gb_ref[n]
            gate = jax.nn.sigmoid(gate)
            log_a = (-RG_C * gate[:, :blk]) * softplus[:, cols]
            a_n = jnp.exp(log_a)
            a_scr[rows, cols] = a_n
            var = 1.0 - a_n * a_n
            mult = jnp.where(var > 0.0, var * lax.rsqrt(var), 0.0)
            b_scr[rows, cols] = mult * gate[:, blk:] * xc_n

        a = a_scr[rows, :].reshape(part // SUBLANES, SUBLANES, tc)
        b = b_scr[rows, :].reshape(part // SUBLANES, SUBLANES, tc)
        row = lax.broadcasted_iota(jnp.int32, a.shape, 1)
        shift = 1
        while shift < SUBLANES:
            keep = row >= shift
            a_prev = pltpu.roll(a, shift, axis=1)
            b_prev = pltpu.roll(b, shift, axis=1)
            b = jnp.where(keep, a * b_prev + b, b)
            a = jnp.where(keep, a * a_prev, a)
            shift *= 2
        a_scr[rows, :] = a.reshape(part, tc)
        b_scr[rows, :] = b.reshape(part, tc)

        for s, start, length in segs:
            lo = start if nseg == 1 else s * tseg
            if nseg > 1 or p == 0:
                h_run = h0_ref[s] if tiles_per_seq == 1 else hcar[c]

            for i in range(length // SUBLANES):
                off = lo + i * SUBLANES
                h = a_scr[off:off + SUBLANES, :] * h_run + b_scr[off:off + SUBLANES, :]
                b_scr[off:off + SUBLANES, :] = h
                h_run = h[SUBLANES - 1:SUBLANES, :]
            if nseg > 1 or p == n_parts - 1:
                hl_ref[s] = h_run
                if tiles_per_seq > 1:
                    hcar[c] = h_run

        out_ref[rows, :] += jnp.dot((b_scr[rows, :] * gelu_gate[p]).astype(BF16), wo_ref[...],
                                    preferred_element_type=F32)

    for s in range(nseg):
        new_hist = rbuf[s, pad + tseg:SUBLANES + tseg, :]
        nh_ref[s] = new_hist
        if tiles_per_seq > 1:
            ccar[c] = new_hist


def _rg_layer(x, hist, h0, t, norm_g, w_in, conv_w, conv_b, gate_w, gate_b, log_lam, w_out):
    m, d = x.shape
    n_seq = m // t
    c_rnn = w_out.shape[0]
    conv_width = conv_w.shape[0]
    blk = c_rnn // RG_BLOCKS
    tc = min(RG_COL_TILE, c_rnn)
    n_c = c_rnn // tc
    assert n_c >= 2
    tseg, nseg, tps = _row_tiling(n_seq, t)
    tm = tseg * nseg
    kern = functools.partial(_rg_kernel, tseg=tseg, nseg=nseg, tiles_per_seq=tps, blk=blk,
                             part=_part_rows(tseg, nseg))
    seq_blk = lambda r, c: (r // tps, 0, c)
    out, new_hist, h_last = pl.pallas_call(
        kern,
        grid=(m // tm, n_c),
        in_specs=[
            pl.BlockSpec(memory_space=pl.ANY),
            pl.BlockSpec((nseg, conv_width - 1, tc), seq_blk),
            pl.BlockSpec((nseg, 1, tc), seq_blk),
            pl.BlockSpec((1, d), lambda r, c: (0, 0)),
            pl.BlockSpec((d, tc), lambda r, c: (0, c)),
            pl.BlockSpec((d, tc), lambda r, c: (0, n_c + c)),
            pl.BlockSpec((conv_width, tc), lambda r, c: (0, c)),
            pl.BlockSpec((1, tc), lambda r, c: (0, c)),
            pl.BlockSpec((tc // blk, blk, 2 * blk), lambda r, c: (c, 0, 0)),
            pl.BlockSpec((tc // blk, 1, 2 * blk), lambda r, c: (c, 0, 0)),
            pl.BlockSpec((1, tc), lambda r, c: (0, c)),
            pl.BlockSpec((tc, d), lambda r, c: (c, 0)),
        ],
        out_specs=[
            pl.BlockSpec((tm, d), lambda r, c: (r, 0)),
            pl.BlockSpec((nseg, conv_width - 1, tc), lambda r, c: (r, 0, c)),
            pl.BlockSpec((nseg, 1, tc), lambda r, c: (r, 0, c)),
        ],
        out_shape=[
            jax.ShapeDtypeStruct((m, d), F32),
            jax.ShapeDtypeStruct((n_seq * tps, conv_width - 1, c_rnn), F32),
            jax.ShapeDtypeStruct((n_seq * tps, 1, c_rnn), F32),
        ],
        scratch_shapes=[
            pltpu.VMEM((tm, d), F32),
            pltpu.SemaphoreType.DMA,
            pltpu.VMEM((tm, d), BF16),
            pltpu.VMEM((nseg, SUBLANES + tseg, tc), F32),
            pltpu.VMEM((tm, tc), F32),
            pltpu.VMEM((tm, tc), F32),
            pltpu.VMEM((tm, tc), F32),
            pltpu.VMEM((n_c, conv_width - 1, tc), F32),
            pltpu.VMEM((n_c, 1, tc), F32),
        ],
        compiler_params=_params(2),
        name="rg_layer",
    )(x, hist, h0, norm_g.reshape(1, d), w_in, w_in, conv_w, conv_b.reshape(1, c_rnn),
      gate_w, gate_b.reshape(RG_BLOCKS, 1, 2 * blk), log_lam.reshape(1, c_rnn), w_out)
    return out, _last_tile(new_hist, n_seq, tps), _last_tile(h_last, n_seq, tps)


def _ffn_kernel(x_hbm, hist_ref, g_ref, wg_ref, wv_ref, cwg_ref, cwv_ref, cbg_ref, cbv_ref, wd_ref,
                fg_ref, out_ref, nh_ref,
                x_buf, x_sem, xn_scr, ubuf, hid_scr, ccar,
                *, tseg, nseg, tiles_per_seq, final_norm, part):
    r = pl.program_id(0)
    c = pl.program_id(1)
    n_c = pl.num_programs(1)
    conv_w = cwg_ref.shape[0]
    tm = tseg * nseg
    tb = r % tiles_per_seq
    pad = SUBLANES - (conv_w - 1)
    halves = ((wg_ref, cwg_ref, cbg_ref), (wv_ref, cwv_ref, cbv_ref))

    def consume():
        x = x_buf[...]
        xn_scr[...] = _rmsnorm_rows(x, g_ref[...]).astype(BF16)
        out_ref[...] = x

    _stream_x_tile(x_hbm, x_buf, x_sem, consume)

    for s in range(nseg):
        for half in range(2):
            def load_state(half=half, s=s):
                for k in range(conv_w - 1):
                    ubuf[half, s, pad + k:pad + k + 1, :] = hist_ref[s, k, half:half + 1, :]

            if tiles_per_seq == 1:
                load_state()
            else:
                pl.when(tb == 0)(load_state)

                @pl.when(tb != 0)
                def _():
                    ubuf[half, s, pad:SUBLANES, :] = ccar[c, half]

    n_parts = tm // part
    for p in range(n_parts):
        xn = xn_scr[p * part:(p + 1) * part, :]
        for half, (w_ref, _, _) in enumerate(halves):
            u = jnp.dot(xn, w_ref[...], preferred_element_type=F32)
            if nseg == 1:
                ubuf[half, 0, SUBLANES + p * part:SUBLANES + (p + 1) * part, :] = u
            else:
                for s in range(nseg):
                    ubuf[half, s, SUBLANES:SUBLANES + tseg, :] = u[s * tseg:(s + 1) * tseg]

    for p in range(n_parts):
        rows = slice(p * part, (p + 1) * part)
        segs = [(0, p * part, part)] if nseg == 1 else [(s, 0, tseg) for s in range(nseg)]
        for s, start, length in segs:
            conv = []
            for half, (_, cw_ref, cb_ref) in enumerate(halves):
                cv = cb_ref[...] + ubuf[half, s, pad + start:pad + start + length, :] * cw_ref[0:1, :]
                for k in range(1, conv_w):
                    lo = pad + start + k
                    cv = cv + ubuf[half, s, lo:lo + length, :] * cw_ref[k:k + 1, :]
                conv.append(cv)
            hid = (jax.nn.gelu(conv[0]) * conv[1]).astype(BF16)
            if nseg > 1:
                hid_scr[s * tseg:(s + 1) * tseg, :] = hid
        if nseg > 1:
            hid = hid_scr[...]
        out_ref[rows, :] += jnp.dot(hid, wd_ref[...], preferred_element_type=F32)

    for s in range(nseg):
        for half in range(2):
            for k in range(conv_w - 1):
                nh_ref[s, k, half:half + 1, :] = ubuf[half, s, pad + tseg + k:pad + tseg + k + 1, :]
            if tiles_per_seq > 1:
                ccar[c, half] = ubuf[half, s, pad + tseg:SUBLANES + tseg, :]

    if final_norm:
        @pl.when(c == n_c - 1)
        def _():
            out_ref[...] = _rmsnorm_rows(out_ref[...], fg_ref[...])


def _ffn_layer(x, hist, t, norm_g, w_up, conv_w, conv_b, w_down, final_g):
    m, d = x.shape
    n_seq = m // t
    d_ff = w_down.shape[0]
    conv_width = conv_w.shape[0]
    tf = FF_COL_TILE
    assert d_ff % tf == 0
    n_c = d_ff // tf
    assert n_c >= 2
    tseg, nseg, tps = _row_tiling(n_seq, t)
    tm = tseg * nseg
    final_norm = final_g is not None
    if final_g is None:
        final_g = jnp.ones((d,), F32)
    kern = functools.partial(_ffn_kernel, tseg=tseg, nseg=nseg, tiles_per_seq=tps,
                             final_norm=final_norm, part=_part_rows(tseg, nseg))
    hist4 = hist.reshape(n_seq, conv_width - 1, 2, d_ff)
    conv_b2 = conv_b.reshape(1, 2 * d_ff)
    out, new_hist = pl.pallas_call(
        kern,
        grid=(m // tm, n_c),
        in_specs=[
            pl.BlockSpec(memory_space=pl.ANY),
            pl.BlockSpec((nseg, conv_width - 1, 2, tf), lambda r, c: (r // tps, 0, 0, c)),
            pl.BlockSpec((1, d), lambda r, c: (0, 0)),
            pl.BlockSpec((d, tf), lambda r, c: (0, c)),
            pl.BlockSpec((d, tf), lambda r, c: (0, n_c + c)),
            pl.BlockSpec((conv_width, tf), lambda r, c: (0, c)),
            pl.BlockSpec((conv_width, tf), lambda r, c: (0, n_c + c)),
            pl.BlockSpec((1, tf), lambda r, c: (0, c)),
            pl.BlockSpec((1, tf), lambda r, c: (0, n_c + c)),
            pl.BlockSpec((tf, d), lambda r, c: (c, 0)),
            pl.BlockSpec((1, d), lambda r, c: (0, 0)),
        ],
        out_specs=[
            pl.BlockSpec((tm, d), lambda r, c: (r, 0)),
            pl.BlockSpec((nseg, conv_width - 1, 2, tf), lambda r, c: (r, 0, 0, c)),
        ],
        out_shape=[
            jax.ShapeDtypeStruct((m, d), F32),
            jax.ShapeDtypeStruct((n_seq * tps, conv_width - 1, 2, d_ff), F32),
        ],
        scratch_shapes=[
            pltpu.VMEM((tm, d), F32),
            pltpu.SemaphoreType.DMA,
            pltpu.VMEM((tm, d), BF16),
            pltpu.VMEM((2, nseg, SUBLANES + tseg, tf), F32),
            pltpu.VMEM((tm if nseg > 1 else SUBLANES, tf), BF16),
            pltpu.VMEM((n_c, 2, conv_width - 1, tf), F32),
        ],
        compiler_params=_params(2),
        name="conv_ffn",
    )(x, hist4, norm_g.reshape(1, d), w_up, w_up, conv_w, conv_w, conv_b2, conv_b2, w_down,
      final_g.reshape(1, d))
    return out, _last_tile(new_hist, n_seq, tps).reshape(n_seq, conv_width - 1, 2 * d_ff)


def _q_kernel(x_ref, g_ref, w_ref, q_ref, xn_ref, *, scale):
    xn = _rmsnorm_rows(x_ref[...], g_ref[...]).astype(BF16)
    xn_ref[...] = xn
    q_ref[...] = (jnp.dot(xn, w_ref[...], preferred_element_type=F32) * scale).astype(BF16)


def _q_proj(x, norm_g, w_qkv, scale):
    m, d = x.shape
    d_att = w_qkv.shape[1] // 3
    tm = min(QKV_ROW_TILE, m)
    return pl.pallas_call(
        functools.partial(_q_kernel, scale=scale),
        grid=(m // tm,),
        in_specs=[
            pl.BlockSpec((tm, d), lambda r: (r, 0)),
            pl.BlockSpec((1, d), lambda r: (0, 0)),
            pl.BlockSpec((d, d_att), lambda r: (0, 0), pipeline_mode=pl.Buffered(1)),
        ],
        out_specs=[
            pl.BlockSpec((tm, d_att), lambda r: (r, 0)),
            pl.BlockSpec((tm, d), lambda r: (r, 0)),
        ],
        out_shape=[
            jax.ShapeDtypeStruct((m, d_att), BF16),
            jax.ShapeDtypeStruct((m, d), BF16),
        ],
        compiler_params=_params(1),
        name="q_proj",
    )(x, norm_g.reshape(1, d), w_qkv)


def _kv_kernel(xn_ref, w_ref, heads_ref, flat_ref):
    y = jnp.dot(xn_ref[...], w_ref[...], preferred_element_type=F32)
    flat_ref[...] = y.astype(BF16)
    heads_ref[...] = pltpu.einshape("m(hd)->mhd", y, h=heads_ref.shape[1])


def _kv_proj(xn, w_qkv, which, n_split):
    m, d = xn.shape
    d_att = w_qkv.shape[1] // 3
    tm = min(QKV_ROW_TILE, m)
    return pl.pallas_call(
        _kv_kernel,
        grid=(m // tm,),
        in_specs=[
            pl.BlockSpec((tm, d), lambda r: (r, 0)),
            pl.BlockSpec((d, d_att), lambda r: (0, which), pipeline_mode=pl.Buffered(1)),
        ],
        out_specs=[
            pl.BlockSpec((tm, n_split, d_att // n_split), lambda r: (r, 0, 0)),
            pl.BlockSpec((tm, d_att), lambda r: (r, 0)),
        ],
        out_shape=[
            jax.ShapeDtypeStruct((m, n_split, d_att // n_split), F32),
            jax.ShapeDtypeStruct((m, d_att), BF16),
        ],
        compiler_params=_params(1),
        name="kv_proj",
    )(xn, w_qkv)


def _diff_lambda(lam_ref, lambda_init):
    lam = lam_ref[...]
    s1 = jnp.sum(lam[0:1] * lam[1:2], axis=-1, keepdims=True)
    s2 = jnp.sum(lam[2:3] * lam[3:4], axis=-1, keepdims=True)
    return jnp.exp(s1) - jnp.exp(s2) + lambda_init


def _head_norm(o, subln_ref, lambda_init):
    return _rmsnorm_rows(o, subln_ref[...]) * (1.0 - lambda_init)


def _nt_dot(a, b):
    return lax.dot_general(a, b, (((1,), (1,)), ((), ())), preferred_element_type=F32)


def _attn_prompt_kernel(q_ref, k_ref, v_ref, lam_ref, subln_ref, o_ref, *, lambda_init, tq):
    t, hw = q_ref.shape
    hd = hw // 2
    lam = _diff_lambda(lam_ref, lambda_init)

    def fold(x, op):
        acc = x[:, 0:LANES]
        for g in range(1, x.shape[1] // LANES):
            acc = op(acc, x[:, g * LANES:(g + 1) * LANES])
        return acc

    def spread(x, width):
        return jnp.concatenate([x] * (width // LANES), axis=1)

    def row_max(s):
        return jnp.broadcast_to(jnp.max(fold(s, jnp.maximum), axis=1, keepdims=True),
                                (s.shape[0], LANES))

    q_chunk = lax.broadcasted_iota(jnp.int32, (tq, tq), 0) // CHUNK
    k_chunk = lax.broadcasted_iota(jnp.int32, (tq, tq), 1) // CHUNK
    visible = k_chunk <= q_chunk
    for i in range(t // tq):
        rows = slice(i * tq, (i + 1) * tq)
        outs = []
        for j in range(2):
            cols = slice(j * hd, (j + 1) * hd)
            q = q_ref[rows, cols]
            s = jnp.where(visible, _nt_dot(q, k_ref[rows, cols]), NEG_INF)
            m = row_max(s)
            p = jnp.exp2(s - spread(m, tq))
            l = fold(p, jnp.add)
            acc = jnp.dot(p.astype(BF16), v_ref[rows, :], preferred_element_type=F32)
            for kb in range(i):
                keys = slice(kb * tq, (kb + 1) * tq)
                s = _nt_dot(q, k_ref[keys, cols])
                m_new = jnp.maximum(m, row_max(s))
                alpha = jnp.exp2(m - m_new)
                p = jnp.exp2(s - spread(m_new, tq))
                l = alpha * l + fold(p, jnp.add)
                acc = spread(alpha, hw) * acc + jnp.dot(p.astype(BF16), v_ref[keys, :],
                                                        preferred_element_type=F32)
                m = m_new
            outs.append(acc / jnp.sum(l, axis=1, keepdims=True))
        o = outs[0] - lam * outs[1]
        o_ref[rows, :] = _head_norm(o, subln_ref, lambda_init).astype(BF16)


def _attn_prompt(q, kb, vb, t, lam_p, subln, lambda_init):
    m, d_att = q.shape
    n_seq = m // t
    hw = d_att // N_HEADS
    tq = min(ATTN_TILE, t)
    assert t % tq == 0 and tq % CHUNK == 0
    seq_head = pl.BlockSpec((t, hw), lambda b, h: (b, h))
    return pl.pallas_call(
        functools.partial(_attn_prompt_kernel, lambda_init=lambda_init, tq=tq),
        grid=(n_seq, N_HEADS),
        in_specs=[
            seq_head, seq_head, seq_head,
            pl.BlockSpec(lam_p.shape, lambda b, h: (0, 0)),
            pl.BlockSpec((1, hw), lambda b, h: (0, 0)),
        ],
        out_specs=seq_head,
        out_shape=jax.ShapeDtypeStruct((m, d_att), BF16),
        compiler_params=_params(2),
        name="attn_prompt",
    )(q, kb, vb, lam_p, subln.reshape(1, hw))


def _attn_sample_kernel(q_ref, ck_ref, cv_ref, kn_ref, vn_ref, lam_ref, subln_ref, o_ref,
                        m_scr, l_scr, acc_scr, *, lambda_init):
    c = pl.program_id(1)
    n_c = pl.num_programs(1)
    t, d_att = q_ref.shape
    hw = d_att // N_HEADS
    hd = hw // 2
    q = q_ref[...]

    @pl.when(c == 0)
    def _():
        m_scr[...] = jnp.full(m_scr.shape, NEG_INF, F32)
        l_scr[...] = jnp.zeros_like(l_scr)
        acc_scr[...] = jnp.zeros_like(acc_scr)

    def attend(k_all, v_all):
        n = k_all.shape[0]
        for h in range(N_HEADS):
            v_h = v_all[:, h * hw:(h + 1) * hw]
            for j in range(2):
                hm = 2 * h + j
                cols = slice(h * hw + j * hd, h * hw + (j + 1) * hd)
                s = _nt_dot(q[:, cols], k_all[:, cols])
                m_old = m_scr[hm]
                m_new = jnp.maximum(m_old, jnp.max(s, axis=1, keepdims=True))
                alpha = jnp.exp2(m_old - m_new)
                p = jnp.exp2(s - m_new[:, 0:1])
                if n % LANES == 0:
                    psum = p[:, 0:LANES]
                    for g in range(1, n // LANES):
                        psum = psum + p[:, g * LANES:(g + 1) * LANES]
                else:
                    psum = jnp.sum(p, axis=1, keepdims=True) * (1.0 / LANES)
                m_scr[hm] = m_new
                l_scr[hm] = alpha * l_scr[hm] + psum
                acc_scr[hm] = (jnp.concatenate([alpha] * (hw // LANES), axis=1) * acc_scr[hm]
                               + jnp.dot(p.astype(BF16), v_h, preferred_element_type=F32))

    attend(pltpu.einshape("mhd->m(hd)", ck_ref[0, 0]).astype(BF16),
           pltpu.einshape("mhd->m(hd)", cv_ref[0, 0]).astype(BF16))

    @pl.when(c == n_c - 1)
    def _():
        attend(kn_ref[...], vn_ref[...])
        lam = _diff_lambda(lam_ref, lambda_init)
        for h in range(N_HEADS):
            outs = [acc_scr[2 * h + j] / jnp.sum(l_scr[2 * h + j], axis=1, keepdims=True)
                    for j in range(2)]
            o = outs[0] - lam * outs[1]
            o_ref[:, h * hw:(h + 1) * hw] = _head_norm(o, subln_ref, lambda_init).astype(BF16)


def _attn_sample(q, cache_k, cache_v, layer, kb, vb, t, lam_p, subln, lambda_init):
    m, d_att = q.shape
    n_seq = m // t
    past = cache_k.shape[2]
    hw = d_att // N_HEADS
    pc = min(SAMPLE_KEY_TILE, past)
    assert past % pc == 0
    new_blk = pl.BlockSpec((t, d_att), lambda b, c: (b, 0))
    cache_blk = lambda arr: pl.BlockSpec((1, 1, pc) + arr.shape[3:], lambda b, c: (layer, b, c, 0, 0))
    return pl.pallas_call(
        functools.partial(_attn_sample_kernel, lambda_init=lambda_init),
        grid=(n_seq, past // pc),
        in_specs=[
            new_blk, cache_blk(cache_k), cache_blk(cache_v), new_blk, new_blk,
            pl.BlockSpec(lam_p.shape, lambda b, c: (0, 0)),
            pl.BlockSpec((1, hw), lambda b, c: (0, 0)),
        ],
        out_specs=new_blk,
        out_shape=jax.ShapeDtypeStruct((m, d_att), BF16),
        scratch_shapes=[
            pltpu.VMEM((2 * N_HEADS, t, LANES), F32),
            pltpu.VMEM((2 * N_HEADS, t, LANES), F32),
            pltpu.VMEM((2 * N_HEADS, t, hw), F32),
        ],
        compiler_params=_params(2),
        name="attn_sample",
    )(q, cache_k, cache_v, kb, vb, lam_p, subln.reshape(1, hw))


def _proj_residual_kernel(o_ref, w_ref, x_ref, out_ref):
    out_ref[...] = x_ref[...] + jnp.dot(o_ref[...], w_ref[...], preferred_element_type=F32)


def _proj_residual(o, w, x):
    m, kdim = o.shape
    n = w.shape[1]
    tm = min(QKV_ROW_TILE, m)
    return pl.pallas_call(
        _proj_residual_kernel,
        grid=(m // tm,),
        in_specs=[
            pl.BlockSpec((tm, kdim), lambda r: (r, 0)),
            pl.BlockSpec((kdim, n), lambda r: (0, 0), pipeline_mode=pl.Buffered(1)),
            pl.BlockSpec((tm, n), lambda r: (r, 0)),
        ],
        out_specs=pl.BlockSpec((tm, n), lambda r: (r, 0)),
        out_shape=jax.ShapeDtypeStruct((m, n), F32),
        compiler_params=_params(1),
        name="attn_out_proj",
    )(o, w, x)


def kernel(x_prompt, x_sample, state_rglru_conv, state_rglru_h, cache_attn_k, cache_attn_v, state_ffn_conv, rg_norm, rg_w_in, rg_conv_w, rg_conv_b, rg_gate_w, rg_gate_b, rg_log_lambda, rg_w_out, at_norm, at_w_qkv, at_lambda, at_subln, at_w_out, ffn_norm, ffn_w_up, ffn_conv_w, ffn_conv_b, ffn_w_down, final_norm):
    bp, tp, d = x_prompt.shape
    bs, ts, _ = x_sample.shape
    depth = ffn_norm.shape[0]
    c_rnn = rg_w_out.shape[1]
    d_att = at_w_out.shape[1]
    head_dim = d_att // (2 * N_HEADS)

    xp = x_prompt.reshape(bp * tp, d)
    xs = x_sample.reshape(bs * ts, d)
    p_rg_conv, p_rg_h, p_k, p_v, p_ffn = [], [], [], [], []
    s_rg_conv, s_rg_h, s_k, s_v, s_ffn = [], [], [], [], []
    for layer in range(depth):
        j = layer // 2
        if layer % 2 == 0:
            w = (rg_norm[j], rg_w_in[j].astype(BF16), rg_conv_w[j], rg_conv_b[j],
                 rg_gate_w[j].astype(BF16), rg_gate_b[j], rg_log_lambda[j], rg_w_out[j].astype(BF16))
            zero_hist = jnp.zeros((bp, rg_conv_w.shape[1] - 1, c_rnn), F32)
            zero_h = jnp.zeros((bp, 1, c_rnn), F32)
            xp, cp, hp = _rg_layer(xp, zero_hist, zero_h, tp, *w)
            xs, cs, hs = _rg_layer(xs, state_rglru_conv[j], state_rglru_h[j].reshape(bs, 1, c_rnn),
                                   ts, *w)
            p_rg_conv.append(cp); p_rg_h.append(hp.reshape(bp, c_rnn))
            s_rg_conv.append(cs); s_rg_h.append(hs.reshape(bs, c_rnn))
        else:
            lambda_init = 0.8 - 0.6 * math.exp(-0.3 * layer)
            w_qkv = at_w_qkv[j].astype(BF16)
            w_out = at_w_out[j].astype(BF16)
            scale = head_dim ** -0.5 * math.log2(math.e)
            qp, xnp = _q_proj(xp, at_norm[j], w_qkv, scale)
            kp, kpb = _kv_proj(xnp, w_qkv, 1, 2 * N_HEADS)
            vp, vpb = _kv_proj(xnp, w_qkv, 2, N_HEADS)
            op = _attn_prompt(qp, kpb, vpb, tp, at_lambda[j], at_subln[j], lambda_init)
            xp = _proj_residual(op, w_out, xp)
            qs, xns = _q_proj(xs, at_norm[j], w_qkv, scale)
            kn, knb = _kv_proj(xns, w_qkv, 1, 2 * N_HEADS)
            vn, vnb = _kv_proj(xns, w_qkv, 2, N_HEADS)
            os_ = _attn_sample(qs, cache_attn_k, cache_attn_v, j, knb, vnb, ts,
                               at_lambda[j], at_subln[j], lambda_init)
            xs = _proj_residual(os_, w_out, xs)
            p_k.append(kp.reshape(bp, tp, 2 * N_HEADS, head_dim))
            p_v.append(vp.reshape(bp, tp, N_HEADS, 2 * head_dim))
            s_k.append(kn.reshape(bs, ts, 2 * N_HEADS, head_dim))
            s_v.append(vn.reshape(bs, ts, N_HEADS, 2 * head_dim))
        last = layer == depth - 1
        fw = (ffn_norm[layer], ffn_w_up[layer].astype(BF16), ffn_conv_w[layer], ffn_conv_b[layer],
              ffn_w_down[layer].astype(BF16), final_norm if last else None)
        zero_ffn = jnp.zeros((bp,) + state_ffn_conv.shape[2:], F32)
        xp, fcp = _ffn_layer(xp, zero_ffn, tp, *fw)
        xs, fcs = _ffn_layer(xs, state_ffn_conv[layer], ts, *fw)
        p_ffn.append(fcp); s_ffn.append(fcs)
    stack = lambda arrs: arrs[0][None] if len(arrs) == 1 else jnp.stack(arrs)
    return (xp.reshape(bp, tp, d), xs.reshape(bs, ts, d),
            stack(p_rg_conv), stack(p_rg_h), stack(p_k), stack(p_v), stack(p_ffn),
            stack(s_rg_conv), stack(s_rg_h), stack(s_k), stack(s_v), stack(s_ffn))
```

```python
import functools
import math

import jax
import jax.numpy as jnp
from jax import lax
from jax.experimental import pallas as pl
from jax.experimental.pallas import tpu as pltpu

F32 = jnp.float32
BF16 = jnp.bfloat16

EPS = 1e-6
NEG_INF = -1e30
CHUNK = 64
N_HEADS = 8
RG_BLOCKS = 16
RG_C = 8.0
SUBLANES = 8
LANES = 128
BF16_SUBLANES = 16
VMEM_LIMIT_BYTES = 58 * 1024 * 1024

LAYER_ROW_TILE = 1024
PART_ROWS = 512
QKV_ROW_TILE = 512
RG_COL_TILE = 512
FF_COL_TILE = 512
ATTN_TILE = 512
SAMPLE_KEY_TILE = 512


def _params(n_axes):
    return pltpu.CompilerParams(
        dimension_semantics=("arbitrary",) * n_axes,
        vmem_limit_bytes=VMEM_LIMIT_BYTES,
    )


def _rmsnorm_rows(x, g):
    ms = jnp.mean(x * x, axis=-1, keepdims=True)
    return (x * lax.rsqrt(ms + EPS)) * g


def _row_tiling(n_seq, t):
    if t >= LAYER_ROW_TILE:
        assert t % LAYER_ROW_TILE == 0
        return LAYER_ROW_TILE, 1, t // LAYER_ROW_TILE
    assert LAYER_ROW_TILE % t == 0 and t % SUBLANES == 0
    nseg = min(LAYER_ROW_TILE // t, n_seq)
    assert n_seq % nseg == 0
    return t, nseg, 1


def _part_rows(tseg, nseg):
    tm = tseg * nseg
    if nseg == 1 and tm % PART_ROWS == 0:
        return PART_ROWS
    return tm


def _last_tile(state, n_seq, tiles_per_seq):
    if tiles_per_seq == 1:
        return state
    return state.reshape((n_seq, tiles_per_seq) + state.shape[1:])[:, -1]


def _cast_row_block(rows, n_steps):
    for br in range(BF16_SUBLANES, rows + 1, BF16_SUBLANES):
        if rows % br == 0 and rows // br <= n_steps:
            return br
    raise ValueError(f"no row block for {rows} rows in {n_steps} steps")


def _cast_specs(cast, n_steps, step_of):
    in_specs, out_specs, out_shapes = [], [], []
    for w3, layer in cast:
        _, rows, cols = w3.shape
        br = _cast_row_block(rows, n_steps)
        blk = lambda *idx, n_blk=rows // br: jnp.minimum(step_of(*idx), n_blk - 1)
        in_specs.append(pl.BlockSpec((1, br, cols), lambda *idx, blk=blk, layer=layer: (layer, blk(*idx), 0)))
        out_specs.append(pl.BlockSpec((br, cols), lambda *idx, blk=blk: (blk(*idx), 0)))
        out_shapes.append(jax.ShapeDtypeStruct((rows, cols), BF16))
    return in_specs, out_specs, out_shapes


def _narrow_blocks(w_refs, wb_refs):
    for w_ref, wb_ref in zip(w_refs, wb_refs):
        wb_ref[...] = w_ref[0].astype(BF16)


def _x_tile_copy(x_hbm, x_buf, sem, r):
    tm = x_buf.shape[0]
    return pltpu.make_async_copy(x_hbm.at[pl.ds(pl.multiple_of(r * tm, tm), tm)], x_buf, sem)


def _stream_x_tile(x_hbm, x_buf, sem, consume):
    r = pl.program_id(0)
    c = pl.program_id(1)
    n_r = pl.num_programs(0)

    @pl.when(c == 0)
    def _():
        @pl.when(r == 0)
        def _():
            _x_tile_copy(x_hbm, x_buf, sem, r).start()

        _x_tile_copy(x_hbm, x_buf, sem, r).wait()
        consume()

    @pl.when(jnp.logical_and(c == 1, r + 1 < n_r))
    def _():
        _x_tile_copy(x_hbm, x_buf, sem, r + 1).start()


def _rg_kernel(*refs, tseg, nseg, tiles_per_seq, blk, part, n_cast):
    (x_hbm, hist_ref, h0_ref, g_ref, wg_ref, wr_ref, cw_ref, cb_ref, gw_ref, gb_ref,
     ll_ref, wo_ref) = refs[:12]
    out_ref, nh_ref, hl_ref = refs[12 + n_cast:15 + n_cast]
    (x_buf, x_sem, xn_scr, rbuf, xc_scr, a_scr, b_scr, ccar, hcar) = refs[15 + 2 * n_cast:]
    _narrow_blocks(refs[12:12 + n_cast], refs[15 + n_cast:15 + 2 * n_cast])
    r = pl.program_id(0)
    c = pl.program_id(1)
    conv_w = cw_ref.shape[0]
    tc = wg_ref.shape[1]
    tm = tseg * nseg
    tb = r % tiles_per_seq
    pad = SUBLANES - (conv_w - 1)

    def consume():
        x = x_buf[...]
        xn_scr[...] = _rmsnorm_rows(x, g_ref[...]).astype(BF16)
        out_ref[...] = x

    _stream_x_tile(x_hbm, x_buf, x_sem, consume)

    for s in range(nseg):
        if tiles_per_seq == 1:
            rbuf[s, pad:SUBLANES, :] = hist_ref[s]
        else:
            @pl.when(tb == 0)
            def _():
                rbuf[s, pad:SUBLANES, :] = hist_ref[s]
                hcar[c] = h0_ref[s]

            @pl.when(tb != 0)
            def _():
                rbuf[s, pad:SUBLANES, :] = ccar[c]

    z = -ll_ref[...]
    softplus = jnp.maximum(z, 0.0) + jnp.log1p(jnp.exp(-jnp.abs(z)))

    n_parts = tm // part
    gelu_gate = []
    for p in range(n_parts):
        xn = xn_scr[p * part:(p + 1) * part, :]
        gelu_gate.append(jax.nn.gelu(jnp.dot(xn, wg_ref[...], preferred_element_type=F32)))
        rec = jnp.dot(xn, wr_ref[...], preferred_element_type=F32)
        if nseg == 1:
            rbuf[0, SUBLANES + p * part:SUBLANES + (p + 1) * part, :] = rec
        else:
            for s in range(nseg):
                rbuf[s, SUBLANES:SUBLANES + tseg, :] = rec[s * tseg:(s + 1) * tseg]

    h_run = None
    for p in range(n_parts):
        rows = slice(p * part, (p + 1) * part)

        segs = [(0, p * part, part)] if nseg == 1 else [(s, 0, tseg) for s in range(nseg)]
        for s, start, length in segs:
            xc = cb_ref[...] + rbuf[s, pad + start:pad + start + length, :] * cw_ref[0:1, :]
            for k in range(1, conv_w):
                lo = pad + start + k
                xc = xc + rbuf[s, lo:lo + length, :] * cw_ref[k:k + 1, :]
            lo = start if nseg == 1 else s * tseg
            xc_scr[lo:lo + length, :] = xc

        for n in range(tc // blk):
            cols = slice(n * blk, (n + 1) * blk)
            xc_n = xc_scr[rows, cols]
            gate = jnp.dot(xc_n.astype(BF16), gw_ref[n], preferred_element_type=F32) + gb_ref[n]
            gate = jax.nn.sigmoid(gate)
            log_a = (-RG_C * gate[:, :blk]) * softplus[:, cols]
            a_n = jnp.exp(log_a)
            a_scr[rows, cols] = a_n
            var = 1.0 - a_n * a_n
            mult = jnp.where(var > 0.0, var * lax.rsqrt(var), 0.0)
            b_scr[rows, cols] = mult * gate[:, blk:] * xc_n

        a = a_scr[rows, :].reshape(part // SUBLANES, SUBLANES, tc)
        b = b_scr[rows, :].reshape(part // SUBLANES, SUBLANES, tc)
        row = lax.broadcasted_iota(jnp.int32, a.shape, 1)
        shift = 1
        while shift < SUBLANES:
            keep = row >= shift
            a_prev = pltpu.roll(a, shift, axis=1)
            b_prev = pltpu.roll(b, shift, axis=1)
            b = jnp.where(keep, a * b_prev + b, b)
            a = jnp.where(keep, a * a_prev, a)
            shift *= 2
        a_scr[rows, :] = a.reshape(part, tc)
        b_scr[rows, :] = b.reshape(part, tc)

        for s, start, length in segs:
            lo = start if nseg == 1 else s * tseg
            if nseg > 1 or p == 0:
                h_run = h0_ref[s] if tiles_per_seq == 1 else hcar[c]

            for i in range(length // SUBLANES):
                off = lo + i * SUBLANES
                h = a_scr[off:off + SUBLANES, :] * h_run + b_scr[off:off + SUBLANES, :]
                b_scr[off:off + SUBLANES, :] = h
                h_run = h[SUBLANES - 1:SUBLANES, :]
            if nseg > 1 or p == n_parts - 1:
                hl_ref[s] = h_run
                if tiles_per_seq > 1:
                    hcar[c] = h_run

        out_ref[rows, :] += jnp.dot((b_scr[rows, :] * gelu_gate[p]).astype(BF16), wo_ref[...],
                                    preferred_element_type=F32)

    for s in range(nseg):
        new_hist = rbuf[s, pad + tseg:SUBLANES + tseg, :]
        nh_ref[s] = new_hist
        if tiles_per_seq > 1:
            ccar[c] = new_hist


def _rg_layer(x, hist, h0, t, norm_g, w_in, conv_w, conv_b, gate_w, gate_b, log_lam, w_out, cast=()):
    m, d = x.shape
    n_seq = m // t
    c_rnn = w_out.shape[0]
    conv_width = conv_w.shape[0]
    blk = c_rnn // RG_BLOCKS
    tc = min(RG_COL_TILE, c_rnn)
    n_c = c_rnn // tc
    assert n_c >= 2
    tseg, nseg, tps = _row_tiling(n_seq, t)
    tm = tseg * nseg
    kern = functools.partial(_rg_kernel, tseg=tseg, nseg=nseg, tiles_per_seq=tps, blk=blk,
                             part=_part_rows(tseg, nseg), n_cast=len(cast))
    seq_blk = lambda r, c: (r // tps, 0, c)
    cast_in, cast_out, cast_shape = _cast_specs(cast, (m // tm) * n_c, lambda r, c: r * n_c + c)
    out, new_hist, h_last, *narrowed = pl.pallas_call(
        kern,
        grid=(m // tm, n_c),
        in_specs=[
            pl.BlockSpec(memory_space=pl.ANY),
            pl.BlockSpec((nseg, conv_width - 1, tc), seq_blk),
            pl.BlockSpec((nseg, 1, tc), seq_blk),
            pl.BlockSpec((1, d), lambda r, c: (0, 0)),
            pl.BlockSpec((d, tc), lambda r, c: (0, c)),
            pl.BlockSpec((d, tc), lambda r, c: (0, n_c + c)),
            pl.BlockSpec((conv_width, tc), lambda r, c: (0, c)),
            pl.BlockSpec((1, tc), lambda r, c: (0, c)),
            pl.BlockSpec((tc // blk, blk, 2 * blk), lambda r, c: (c, 0, 0)),
            pl.BlockSpec((tc // blk, 1, 2 * blk), lambda r, c: (c, 0, 0)),
            pl.BlockSpec((1, tc), lambda r, c: (0, c)),
            pl.BlockSpec((tc, d), lambda r, c: (c, 0)),
        ] + cast_in,
        out_specs=[
            pl.BlockSpec((tm, d), lambda r, c: (r, 0)),
            pl.BlockSpec((nseg, conv_width - 1, tc), lambda r, c: (r, 0, c)),
            pl.BlockSpec((nseg, 1, tc), lambda r, c: (r, 0, c)),
        ] + cast_out,
        out_shape=[
            jax.ShapeDtypeStruct((m, d), F32),
            jax.ShapeDtypeStruct((n_seq * tps, conv_width - 1, c_rnn), F32),
            jax.ShapeDtypeStruct((n_seq * tps, 1, c_rnn), F32),
        ] + cast_shape,
        scratch_shapes=[
            pltpu.VMEM((tm, d), F32),
            pltpu.SemaphoreType.DMA,
            pltpu.VMEM((tm, d), BF16),
            pltpu.VMEM((nseg, SUBLANES + tseg, tc), F32),
            pltpu.VMEM((tm, tc), F32),
            pltpu.VMEM((tm, tc), F32),
            pltpu.VMEM((tm, tc), F32),
            pltpu.VMEM((n_c, conv_width - 1, tc), F32),
            pltpu.VMEM((n_c, 1, tc), F32),
        ],
        compiler_params=_params(2),
        name="rg_layer",
    )(x, hist, h0, norm_g.reshape(1, d), w_in, w_in, conv_w, conv_b.reshape(1, c_rnn),
      gate_w, gate_b.reshape(RG_BLOCKS, 1, 2 * blk), log_lam.reshape(1, c_rnn), w_out,
      *[w3 for w3, _ in cast])
    return out, _last_tile(new_hist, n_seq, tps), _last_tile(h_last, n_seq, tps), narrowed


def _ffn_kernel(x_hbm, hist_ref, g_ref, wg_ref, wv_ref, cwg_ref, cwv_ref, cbg_ref, cbv_ref, wd_ref,
                fg_ref, out_ref, nh_ref,
                x_buf, x_sem, xn_scr, ubuf, hid_scr, ccar,
                *, tseg, nseg, tiles_per_seq, final_norm, part):
    r = pl.program_id(0)
    c = pl.program_id(1)
    n_c = pl.num_programs(1)
    conv_w = cwg_ref.shape[0]
    tm = tseg * nseg
    tb = r % tiles_per_seq
    pad = SUBLANES - (conv_w - 1)
    halves = ((wg_ref, cwg_ref, cbg_ref), (wv_ref, cwv_ref, cbv_ref))

    def consume():
        x = x_buf[...]
        xn_scr[...] = _rmsnorm_rows(x, g_ref[...]).astype(BF16)
        out_ref[...] = x

    _stream_x_tile(x_hbm, x_buf, x_sem, consume)

    for s in range(nseg):
        for half in range(2):
            def load_state(half=half, s=s):
                for k in range(conv_w - 1):
                    ubuf[half, s, pad + k:pad + k + 1, :] = hist_ref[s, k, half:half + 1, :]

            if tiles_per_seq == 1:
                load_state()
            else:
                pl.when(tb == 0)(load_state)

                @pl.when(tb != 0)
                def _():
                    ubuf[half, s, pad:SUBLANES, :] = ccar[c, half]

    n_parts = tm // part
    for p in range(n_parts):
        xn = xn_scr[p * part:(p + 1) * part, :]
        for half, (w_ref, _, _) in enumerate(halves):
            u = jnp.dot(xn, w_ref[...], preferred_element_type=F32)
            if nseg == 1:
                ubuf[half, 0, SUBLANES + p * part:SUBLANES + (p + 1) * part, :] = u
            else:
                for s in range(nseg):
                    ubuf[half, s, SUBLANES:SUBLANES + tseg, :] = u[s * tseg:(s + 1) * tseg]

    for p in range(n_parts):
        rows = slice(p * part, (p + 1) * part)
        segs = [(0, p * part, part)] if nseg == 1 else [(s, 0, tseg) for s in range(nseg)]
        for s, start, length in segs:
            conv = []
            for half, (_, cw_ref, cb_ref) in enumerate(halves):
                cv = cb_ref[...] + ubuf[half, s, pad + start:pad + start + length, :] * cw_ref[0:1, :]
                for k in range(1, conv_w):
                    lo = pad + start + k
                    cv = cv + ubuf[half, s, lo:lo + length, :] * cw_ref[k:k + 1, :]
                conv.append(cv)
            hid = (jax.nn.gelu(conv[0]) * conv[1]).astype(BF16)
            if nseg > 1:
                hid_scr[s * tseg:(s + 1) * tseg, :] = hid
        if nseg > 1:
            hid = hid_scr[...]
        out_ref[rows, :] += jnp.dot(hid, wd_ref[...], preferred_element_type=F32)

    for s in range(nseg):
        for half in range(2):
            for k in range(conv_w - 1):
                nh_ref[s, k, half:half + 1, :] = ubuf[half, s, pad + tseg + k:pad + tseg + k + 1, :]
            if tiles_per_seq > 1:
                ccar[c, half] = ubuf[half, s, pad + tseg:SUBLANES + tseg, :]

    if final_norm:
        @pl.when(c == n_c - 1)
        def _():
            out_ref[...] = _rmsnorm_rows(out_ref[...], fg_ref[...])


def _ffn_layer(x, hist, t, norm_g, w_up, conv_w, conv_b, w_down, final_g):
    m, d = x.shape
    n_seq = m // t
    d_ff = w_down.shape[0]
    conv_width = conv_w.shape[0]
    tf = FF_COL_TILE
    assert d_ff % tf == 0
    n_c = d_ff // tf
    assert n_c >= 2
    tseg, nseg, tps = _row_tiling(n_seq, t)
    tm = tseg * nseg
    final_norm = final_g is not None
    if final_g is None:
        final_g = jnp.ones((d,), F32)
    kern = functools.partial(_ffn_kernel, tseg=tseg, nseg=nseg, tiles_per_seq=tps,
                             final_norm=final_norm, part=_part_rows(tseg, nseg))
    hist4 = hist.reshape(n_seq, conv_width - 1, 2, d_ff)
    conv_b2 = conv_b.reshape(1, 2 * d_ff)
    out, new_hist = pl.pallas_call(
        kern,
        grid=(m // tm, n_c),
        in_specs=[
            pl.BlockSpec(memory_space=pl.ANY),
            pl.BlockSpec((nseg, conv_width - 1, 2, tf), lambda r, c: (r // tps, 0, 0, c)),
            pl.BlockSpec((1, d), lambda r, c: (0, 0)),
            pl.BlockSpec((d, tf), lambda r, c: (0, c)),
            pl.BlockSpec((d, tf), lambda r, c: (0, n_c + c)),
            pl.BlockSpec((conv_width, tf), lambda r, c: (0, c)),
            pl.BlockSpec((conv_width, tf), lambda r, c: (0, n_c + c)),
            pl.BlockSpec((1, tf), lambda r, c: (0, c)),
            pl.BlockSpec((1, tf), lambda r, c: (0, n_c + c)),
            pl.BlockSpec((tf, d), lambda r, c: (c, 0)),
            pl.BlockSpec((1, d), lambda r, c: (0, 0)),
        ],
        out_specs=[
            pl.BlockSpec((tm, d), lambda r, c: (r, 0)),
            pl.BlockSpec((nseg, conv_width - 1, 2, tf), lambda r, c: (r, 0, 0, c)),
        ],
        out_shape=[
            jax.ShapeDtypeStruct((m, d), F32),
            jax.ShapeDtypeStruct((n_seq * tps, conv_width - 1, 2, d_ff), F32),
        ],
        scratch_shapes=[
            pltpu.VMEM((tm, d), F32),
            pltpu.SemaphoreType.DMA,
            pltpu.VMEM((tm, d), BF16),
            pltpu.VMEM((2, nseg, SUBLANES + tseg, tf), F32),
            pltpu.VMEM((tm if nseg > 1 else SUBLANES, tf), BF16),
            pltpu.VMEM((n_c, 2, conv_width - 1, tf), F32),
        ],
        compiler_params=_params(2),
        name="conv_ffn",
    )(x, hist4, norm_g.reshape(1, d), w_up, w_up, conv_w, conv_w, conv_b2, conv_b2, w_down,
      final_g.reshape(1, d))
    return out, _last_tile(new_hist, n_seq, tps).reshape(n_seq, conv_width - 1, 2 * d_ff)


def _q_kernel(x_ref, g_ref, w_ref, q_ref, xn_ref, *, scale):
    xn = _rmsnorm_rows(x_ref[...], g_ref[...]).astype(BF16)
    xn_ref[...] = xn
    q_ref[...] = (jnp.dot(xn, w_ref[...], preferred_element_type=F32) * scale).astype(BF16)


def _q_proj(x, norm_g, w_qkv, scale):
    m, d = x.shape
    d_att = w_qkv.shape[1] // 3
    tm = min(QKV_ROW_TILE, m)
    return pl.pallas_call(
        functools.partial(_q_kernel, scale=scale),
        grid=(m // tm,),
        in_specs=[
            pl.BlockSpec((tm, d), lambda r: (r, 0)),
            pl.BlockSpec((1, d), lambda r: (0, 0)),
            pl.BlockSpec((d, d_att), lambda r: (0, 0), pipeline_mode=pl.Buffered(1)),
        ],
        out_specs=[
            pl.BlockSpec((tm, d_att), lambda r: (r, 0)),
            pl.BlockSpec((tm, d), lambda r: (r, 0)),
        ],
        out_shape=[
            jax.ShapeDtypeStruct((m, d_att), BF16),
            jax.ShapeDtypeStruct((m, d), BF16),
        ],
        compiler_params=_params(1),
        name="q_proj",
    )(x, norm_g.reshape(1, d), w_qkv)


def _kv_kernel(xn_ref, w_ref, heads_ref, flat_ref):
    y = jnp.dot(xn_ref[...], w_ref[...], preferred_element_type=F32)
    flat_ref[...] = y.astype(BF16)
    heads_ref[...] = pltpu.einshape("m(hd)->mhd", y, h=heads_ref.shape[1])


def _kv_proj(xn, w_qkv, which, n_split):
    m, d = xn.shape
    d_att = w_qkv.shape[1] // 3
    tm = min(QKV_ROW_TILE, m)
    return pl.pallas_call(
        _kv_kernel,
        grid=(m // tm,),
        in_specs=[
            pl.BlockSpec((tm, d), lambda r: (r, 0)),
            pl.BlockSpec((d, d_att), lambda r: (0, which), pipeline_mode=pl.Buffered(1)),
        ],
        out_specs=[
            pl.BlockSpec((tm, n_split, d_att // n_split), lambda r: (r, 0, 0)),
            pl.BlockSpec((tm, d_att), lambda r: (r, 0)),
        ],
        out_shape=[
            jax.ShapeDtypeStruct((m, n_split, d_att // n_split), F32),
            jax.ShapeDtypeStruct((m, d_att), BF16),
        ],
        compiler_params=_params(1),
        name="kv_proj",
    )(xn, w_qkv)


def _diff_lambda(lam_ref, lambda_init):
    lam = lam_ref[...]
    s1 = jnp.sum(lam[0:1] * lam[1:2], axis=-1, keepdims=True)
    s2 = jnp.sum(lam[2:3] * lam[3:4], axis=-1, keepdims=True)
    return jnp.exp(s1) - jnp.exp(s2) + lambda_init


def _head_norm(o, subln_ref, lambda_init):
    return _rmsnorm_rows(o, subln_ref[...]) * (1.0 - lambda_init)


def _nt_dot(a, b):
    return lax.dot_general(a, b, (((1,), (1,)), ((), ())), preferred_element_type=F32)


def _attn_prompt_kernel(q_ref, k_ref, v_ref, lam_ref, subln_ref, *refs, lambda_init, tq, n_cast):
    o_ref = refs[n_cast]
    _narrow_blocks(refs[:n_cast], refs[n_cast + 1:])
    t, hw = q_ref.shape
    hd = hw // 2
    lam = _diff_lambda(lam_ref, lambda_init)

    def fold(x, op):
        acc = x[:, 0:LANES]
        for g in range(1, x.shape[1] // LANES):
            acc = op(acc, x[:, g * LANES:(g + 1) * LANES])
        return acc

    def spread(x, width):
        return jnp.concatenate([x] * (width // LANES), axis=1)

    def row_max(s):
        return jnp.broadcast_to(jnp.max(fold(s, jnp.maximum), axis=1, keepdims=True),
                                (s.shape[0], LANES))

    q_chunk = lax.broadcasted_iota(jnp.int32, (tq, tq), 0) // CHUNK
    k_chunk = lax.broadcasted_iota(jnp.int32, (tq, tq), 1) // CHUNK
    visible = k_chunk <= q_chunk
    for i in range(t // tq):
        rows = slice(i * tq, (i + 1) * tq)
        outs = []
        for j in range(2):
            cols = slice(j * hd, (j + 1) * hd)
            q = q_ref[rows, cols]
            s = jnp.where(visible, _nt_dot(q, k_ref[rows, cols]), NEG_INF)
            m = row_max(s)
            p = jnp.exp2(s - spread(m, tq))
            l = fold(p, jnp.add)
            acc = jnp.dot(p.astype(BF16), v_ref[rows, :], preferred_element_type=F32)
            for kb in range(i):
                keys = slice(kb * tq, (kb + 1) * tq)
                s = _nt_dot(q, k_ref[keys, cols])
                m_new = jnp.maximum(m, row_max(s))
                alpha = jnp.exp2(m - m_new)
                p = jnp.exp2(s - spread(m_new, tq))
                l = alpha * l + fold(p, jnp.add)
                acc = spread(alpha, hw) * acc + jnp.dot(p.astype(BF16), v_ref[keys, :],
                                                        preferred_element_type=F32)
                m = m_new
            outs.append(acc / jnp.sum(l, axis=1, keepdims=True))
        o = outs[0] - lam * outs[1]
        o_ref[rows, :] = _head_norm(o, subln_ref, lambda_init).astype(BF16)


def _attn_prompt(q, kb, vb, t, lam_p, subln, lambda_init, cast=()):
    m, d_att = q.shape
    n_seq = m // t
    hw = d_att // N_HEADS
    tq = min(ATTN_TILE, t)
    assert t % tq == 0 and tq % CHUNK == 0
    seq_head = pl.BlockSpec((t, hw), lambda b, h: (b, h))
    cast_in, cast_out, cast_shape = _cast_specs(cast, n_seq * N_HEADS, lambda b, h: b * N_HEADS + h)
    o, *narrowed = pl.pallas_call(
        functools.partial(_attn_prompt_kernel, lambda_init=lambda_init, tq=tq, n_cast=len(cast)),
        grid=(n_seq, N_HEADS),
        in_specs=[
            seq_head, seq_head, seq_head,
            pl.BlockSpec(lam_p.shape, lambda b, h: (0, 0)),
            pl.BlockSpec((1, hw), lambda b, h: (0, 0)),
        ] + cast_in,
        out_specs=[seq_head] + cast_out,
        out_shape=[jax.ShapeDtypeStruct((m, d_att), BF16)] + cast_shape,
        compiler_params=_params(2),
        name="attn_prompt",
    )(q, kb, vb, lam_p, subln.reshape(1, hw), *[w3 for w3, _ in cast])
    return o, narrowed


def _attn_sample_kernel(q_ref, ck_ref, cv_ref, kn_ref, vn_ref, lam_ref, subln_ref, o_ref,
                        m_scr, l_scr, acc_scr, *, lambda_init):
    c = pl.program_id(1)
    n_c = pl.num_programs(1)
    t, d_att = q_ref.shape
    hw = d_att // N_HEADS
    hd = hw // 2
    q = q_ref[...]

    @pl.when(c == 0)
    def _():
        m_scr[...] = jnp.full(m_scr.shape, NEG_INF, F32)
        l_scr[...] = jnp.zeros_like(l_scr)
        acc_scr[...] = jnp.zeros_like(acc_scr)

    def attend(k_all, v_all):
        n = k_all.shape[0]
        for h in range(N_HEADS):
            v_h = v_all[:, h * hw:(h + 1) * hw]
            for j in range(2):
                hm = 2 * h + j
                cols = slice(h * hw + j * hd, h * hw + (j + 1) * hd)
                s = _nt_dot(q[:, cols], k_all[:, cols])
                m_old = m_scr[hm]
                m_new = jnp.maximum(m_old, jnp.max(s, axis=1, keepdims=True))
                alpha = jnp.exp2(m_old - m_new)
                p = jnp.exp2(s - m_new[:, 0:1])
                if n % LANES == 0:
                    psum = p[:, 0:LANES]
                    for g in range(1, n // LANES):
                        psum = psum + p[:, g * LANES:(g + 1) * LANES]
                else:
                    psum = jnp.sum(p, axis=1, keepdims=True) * (1.0 / LANES)
                m_scr[hm] = m_new
                l_scr[hm] = alpha * l_scr[hm] + psum
                acc_scr[hm] = (jnp.concatenate([alpha] * (hw // LANES), axis=1) * acc_scr[hm]
                               + jnp.dot(p.astype(BF16), v_h, preferred_element_type=F32))

    attend(pltpu.einshape("mhd->m(hd)", ck_ref[0, 0]).astype(BF16),
           pltpu.einshape("mhd->m(hd)", cv_ref[0, 0]).astype(BF16))

    @pl.when(c == n_c - 1)
    def _():
        attend(kn_ref[...], vn_ref[...])
        lam = _diff_lambda(lam_ref, lambda_init)
        for h in range(N_HEADS):
            outs = [acc_scr[2 * h + j] / jnp.sum(l_scr[2 * h + j], axis=1, keepdims=True)
                    for j in range(2)]
            o = outs[0] - lam * outs[1]
            o_ref[:, h * hw:(h + 1) * hw] = _head_norm(o, subln_ref, lambda_init).astype(BF16)


def _attn_sample(q, cache_k, cache_v, layer, kb, vb, t, lam_p, subln, lambda_init):
    m, d_att = q.shape
    n_seq = m // t
    past = cache_k.shape[2]
    hw = d_att // N_HEADS
    pc = min(SAMPLE_KEY_TILE, past)
    assert past % pc == 0
    new_blk = pl.BlockSpec((t, d_att), lambda b, c: (b, 0))
    cache_blk = lambda arr: pl.BlockSpec((1, 1, pc) + arr.shape[3:], lambda b, c: (layer, b, c, 0, 0))
    return pl.pallas_call(
        functools.partial(_attn_sample_kernel, lambda_init=lambda_init),
        grid=(n_seq, past // pc),
        in_specs=[
            new_blk, cache_blk(cache_k), cache_blk(cache_v), new_blk, new_blk,
            pl.BlockSpec(lam_p.shape, lambda b, c: (0, 0)),
            pl.BlockSpec((1, hw), lambda b, c: (0, 0)),
        ],
        out_specs=new_blk,
        out_shape=jax.ShapeDtypeStruct((m, d_att), BF16),
        scratch_shapes=[
            pltpu.VMEM((2 * N_HEADS, t, LANES), F32),
            pltpu.VMEM((2 * N_HEADS, t, LANES), F32),
            pltpu.VMEM((2 * N_HEADS, t, hw), F32),
        ],
        compiler_params=_params(2),
        name="attn_sample",
    )(q, cache_k, cache_v, kb, vb, lam_p, subln.reshape(1, hw))


def _proj_residual_kernel(o_ref, w_ref, x_ref, out_ref):
    out_ref[...] = x_ref[...] + jnp.dot(o_ref[...], w_ref[...], preferred_element_type=F32)


def _proj_residual(o, w, x):
    m, kdim = o.shape
    n = w.shape[1]
    tm = min(QKV_ROW_TILE, m)
    return pl.pallas_call(
        _proj_residual_kernel,
        grid=(m // tm,),
        in_specs=[
            pl.BlockSpec((tm, kdim), lambda r: (r, 0)),
            pl.BlockSpec((kdim, n), lambda r: (0, 0), pipeline_mode=pl.Buffered(1)),
            pl.BlockSpec((tm, n), lambda r: (r, 0)),
        ],
        out_specs=pl.BlockSpec((tm, n), lambda r: (r, 0)),
        out_shape=jax.ShapeDtypeStruct((m, n), F32),
        compiler_params=_params(1),
        name="attn_out_proj",
    )(o, w, x)


def kernel(x_prompt, x_sample, state_rglru_conv, state_rglru_h, cache_attn_k, cache_attn_v, state_ffn_conv, rg_norm, rg_w_in, rg_conv_w, rg_conv_b, rg_gate_w, rg_gate_b, rg_log_lambda, rg_w_out, at_norm, at_w_qkv, at_lambda, at_subln, at_w_out, ffn_norm, ffn_w_up, ffn_conv_w, ffn_conv_b, ffn_w_down, final_norm):
    bp, tp, d = x_prompt.shape
    bs, ts, _ = x_sample.shape
    depth = ffn_norm.shape[0]
    c_rnn = rg_w_out.shape[1]
    d_att = at_w_out.shape[1]
    head_dim = d_att // (2 * N_HEADS)

    xp = x_prompt.reshape(bp * tp, d)
    xs = x_sample.reshape(bs * ts, d)
    p_rg_conv, p_rg_h, p_k, p_v, p_ffn = [], [], [], [], []
    s_rg_conv, s_rg_h, s_k, s_v, s_ffn = [], [], [], [], []
    ffn_narrowed = {}
    for layer in range(depth):
        j = layer // 2
        if layer % 2 == 0:
            w = (rg_norm[j], rg_w_in[j].astype(BF16), rg_conv_w[j], rg_conv_b[j],
                 rg_gate_w[j].astype(BF16), rg_gate_b[j], rg_log_lambda[j], rg_w_out[j].astype(BF16))
            zero_hist = jnp.zeros((bp, rg_conv_w.shape[1] - 1, c_rnn), F32)
            zero_h = jnp.zeros((bp, 1, c_rnn), F32)
            xp, cp, hp, ffn_narrowed[layer] = _rg_layer(
                xp, zero_hist, zero_h, tp, *w, cast=((ffn_w_up, layer), (ffn_w_down, layer)))
            xs, cs, hs, _ = _rg_layer(xs, state_rglru_conv[j],
                                      state_rglru_h[j].reshape(bs, 1, c_rnn), ts, *w)
            p_rg_conv.append(cp); p_rg_h.append(hp.reshape(bp, c_rnn))
            s_rg_conv.append(cs); s_rg_h.append(hs.reshape(bs, c_rnn))
        else:
            lambda_init = 0.8 - 0.6 * math.exp(-0.3 * layer)
            w_qkv = at_w_qkv[j].astype(BF16)
            w_out = at_w_out[j].astype(BF16)
            scale = head_dim ** -0.5 * math.log2(math.e)
            qp, xnp = _q_proj(xp, at_norm[j], w_qkv, scale)
            kp, kpb = _kv_proj(xnp, w_qkv, 1, 2 * N_HEADS)
            vp, vpb = _kv_proj(xnp, w_qkv, 2, N_HEADS)
            op, ffn_narrowed[layer] = _attn_prompt(
                qp, kpb, vpb, tp, at_lambda[j], at_subln[j], lambda_init,
                cast=((ffn_w_up, layer), (ffn_w_down, layer)))
            xp = _proj_residual(op, w_out, xp)
            qs, xns = _q_proj(xs, at_norm[j], w_qkv, scale)
            kn, knb = _kv_proj(xns, w_qkv, 1, 2 * N_HEADS)
            vn, vnb = _kv_proj(xns, w_qkv, 2, N_HEADS)
            os_ = _attn_sample(qs, cache_attn_k, cache_attn_v, j, knb, vnb, ts,
                               at_lambda[j], at_subln[j], lambda_init)
            xs = _proj_residual(os_, w_out, xs)
            p_k.append(kp.reshape(bp, tp, 2 * N_HEADS, head_dim))
            p_v.append(vp.reshape(bp, tp, N_HEADS, 2 * head_dim))
            s_k.append(kn.reshape(bs, ts, 2 * N_HEADS, head_dim))
            s_v.append(vn.reshape(bs, ts, N_HEADS, 2 * head_dim))
        last = layer == depth - 1
        w_up, w_down = ffn_narrowed.get(layer) or (ffn_w_up[layer].astype(BF16),
                                                   ffn_w_down[layer].astype(BF16))
        fw = (ffn_norm[layer], w_up, ffn_conv_w[layer], ffn_conv_b[layer], w_down,
              final_norm if last else None)
        zero_ffn = jnp.zeros((bp,) + state_ffn_conv.shape[2:], F32)
        xp, fcp = _ffn_layer(xp, zero_ffn, tp, *fw)
        xs, fcs = _ffn_layer(xs, state_ffn_conv[layer], ts, *fw)
        p_ffn.append(fcp); s_ffn.append(fcs)
    stack = lambda arrs: arrs[0][None] if len(arrs) == 1 else jnp.stack(arrs)
    return (xp.reshape(bp, tp, d), xs.reshape(bs, ts, d),
            stack(p_rg_conv), stack(p_rg_h), stack(p_k), stack(p_v), stack(p_ffn),
            stack(s_rg_conv), stack(s_rg_h), stack(s_k), stack(s_v), stack(s_ffn))
```

```python
import functools
import math

import jax
import jax.numpy as jnp
from jax import lax
from jax.experimental import pallas as pl
from jax.experimental.pallas import tpu as pltpu

F32 = jnp.float32
BF16 = jnp.bfloat16

EPS = 1e-6
NEG_INF = -1e30
CHUNK = 64
N_HEADS = 8
RG_BLOCKS = 16
RG_C = 8.0
SUBLANES = 8
LANES = 128
BF16_SUBLANES = 16
VMEM_LIMIT_BYTES = 58 * 1024 * 1024

LAYER_ROW_TILE = 1024
PART_ROWS = 512
QKV_ROW_TILE = 512
KV_ROW_TILE = 512
RG_COL_TILE = 512
FF_COL_TILE = 512
ATTN_TILE = 512
SAMPLE_KEY_TILE = 512


def _params(n_axes):
    return pltpu.CompilerParams(
        dimension_semantics=("arbitrary",) * n_axes,
        vmem_limit_bytes=VMEM_LIMIT_BYTES,
    )


def _rmsnorm_rows(x, g):
    ms = jnp.mean(x * x, axis=-1, keepdims=True)
    return (x * lax.rsqrt(ms + EPS)) * g


def _row_tiling(n_seq, t):
    if t >= LAYER_ROW_TILE:
        assert t % LAYER_ROW_TILE == 0
        return LAYER_ROW_TILE, 1, t // LAYER_ROW_TILE
    assert LAYER_ROW_TILE % t == 0 and t % SUBLANES == 0
    nseg = min(LAYER_ROW_TILE // t, n_seq)
    assert n_seq % nseg == 0
    return t, nseg, 1


def _part_rows(tseg, nseg):
    tm = tseg * nseg
    if nseg == 1 and tm % PART_ROWS == 0:
        return PART_ROWS
    return tm


def _last_tile(state, n_seq, tiles_per_seq):
    if tiles_per_seq == 1:
        return state
    return state.reshape((n_seq, tiles_per_seq) + state.shape[1:])[:, -1]


def _cast_row_block(rows, n_steps):
    for br in range(BF16_SUBLANES, rows + 1, BF16_SUBLANES):
        if rows % br == 0 and rows // br <= n_steps:
            return br
    raise ValueError(f"no row block for {rows} rows in {n_steps} steps")


def _cast_specs(cast, n_steps, step_of):
    in_specs, out_specs, out_shapes = [], [], []
    for w3, layer in cast:
        _, rows, cols = w3.shape
        br = _cast_row_block(rows, n_steps)
        blk = lambda *idx, n_blk=rows // br: jnp.minimum(step_of(*idx), n_blk - 1)
        in_specs.append(pl.BlockSpec((1, br, cols), lambda *idx, blk=blk, layer=layer: (layer, blk(*idx), 0)))
        out_specs.append(pl.BlockSpec((br, cols), lambda *idx, blk=blk: (blk(*idx), 0)))
        out_shapes.append(jax.ShapeDtypeStruct((rows, cols), BF16))
    return in_specs, out_specs, out_shapes


def _narrow_blocks(w_refs, wb_refs):
    for w_ref, wb_ref in zip(w_refs, wb_refs):
        wb_ref[...] = w_ref[0].astype(BF16)


def _x_tile_copy(x_hbm, x_buf, sem, r):
    tm = x_buf.shape[0]
    return pltpu.make_async_copy(x_hbm.at[pl.ds(pl.multiple_of(r * tm, tm), tm)], x_buf, sem)


def _stream_x_tile(x_hbm, x_buf, sem, consume):
    r = pl.program_id(0)
    c = pl.program_id(1)
    n_r = pl.num_programs(0)

    @pl.when(c == 0)
    def _():
        @pl.when(r == 0)
        def _():
            _x_tile_copy(x_hbm, x_buf, sem, r).start()

        _x_tile_copy(x_hbm, x_buf, sem, r).wait()
        consume()

    @pl.when(jnp.logical_and(c == 1, r + 1 < n_r))
    def _():
        _x_tile_copy(x_hbm, x_buf, sem, r + 1).start()


def _rg_kernel(*refs, tseg, nseg, tiles_per_seq, blk, part, n_cast):
    (x_hbm, hist_ref, h0_ref, g_ref, wg_ref, wr_ref, cw_ref, cb_ref, gw_ref, gb_ref,
     ll_ref, wo_ref) = refs[:12]
    out_ref, nh_ref, hl_ref = refs[12 + n_cast:15 + n_cast]
    (x_buf, x_sem, xn_scr, rbuf, xc_scr, a_scr, b_scr, ccar, hcar) = refs[15 + 2 * n_cast:]
    _narrow_blocks(refs[12:12 + n_cast], refs[15 + n_cast:15 + 2 * n_cast])
    r = pl.program_id(0)
    c = pl.program_id(1)
    conv_w = cw_ref.shape[0]
    tc = wg_ref.shape[1]
    tm = tseg * nseg
    tb = r % tiles_per_seq
    pad = SUBLANES - (conv_w - 1)

    def consume():
        x = x_buf[...]
        xn_scr[...] = _rmsnorm_rows(x, g_ref[...]).astype(BF16)
        out_ref[...] = x

    _stream_x_tile(x_hbm, x_buf, x_sem, consume)

    for s in range(nseg):
        if tiles_per_seq == 1:
            rbuf[s, pad:SUBLANES, :] = hist_ref[s]
        else:
            @pl.when(tb == 0)
            def _():
                rbuf[s, pad:SUBLANES, :] = hist_ref[s]
                hcar[c] = h0_ref[s]

            @pl.when(tb != 0)
            def _():
                rbuf[s, pad:SUBLANES, :] = ccar[c]

    z = -ll_ref[...]
    softplus = jnp.maximum(z, 0.0) + jnp.log1p(jnp.exp(-jnp.abs(z)))

    n_parts = tm // part
    gelu_gate = []
    for p in range(n_parts):
        xn = xn_scr[p * part:(p + 1) * part, :]
        gelu_gate.append(jax.nn.gelu(jnp.dot(xn, wg_ref[...], preferred_element_type=F32)))
        rec = jnp.dot(xn, wr_ref[...], preferred_element_type=F32)
        if nseg == 1:
            rbuf[0, SUBLANES + p * part:SUBLANES + (p + 1) * part, :] = rec
        else:
            for s in range(nseg):
                rbuf[s, SUBLANES:SUBLANES + tseg, :] = rec[s * tseg:(s + 1) * tseg]

    h_run = None
    for p in range(n_parts):
        rows = slice(p * part, (p + 1) * part)

        segs = [(0, p * part, part)] if nseg == 1 else [(s, 0, tseg) for s in range(nseg)]
        for s, start, length in segs:
            xc = cb_ref[...] + rbuf[s, pad + start:pad + start + length, :] * cw_ref[0:1, :]
            for k in range(1, conv_w):
                lo = pad + start + k
                xc = xc + rbuf[s, lo:lo + length, :] * cw_ref[k:k + 1, :]
            lo = start if nseg == 1 else s * tseg
            xc_scr[lo:lo + length, :] = xc

        for n in range(tc // blk):
            cols = slice(n * blk, (n + 1) * blk)
            xc_n = xc_scr[rows, cols]
            gate = jnp.dot(xc_n.astype(BF16), gw_ref[n], preferred_element_type=F32) + gb_ref[n]
            gate = jax.nn.sigmoid(gate)
            log_a = (-RG_C * gate[:, :blk]) * softplus[:, cols]
            a_n = jnp.exp(log_a)
            a_scr[rows, cols] = a_n
            var = 1.0 - a_n * a_n
            mult = jnp.where(var > 0.0, var * lax.rsqrt(var), 0.0)
            b_scr[rows, cols] = mult * gate[:, blk:] * xc_n

        a = a_scr[rows, :].reshape(part // SUBLANES, SUBLANES, tc)
        b = b_scr[rows, :].reshape(part // SUBLANES, SUBLANES, tc)
        row = lax.broadcasted_iota(jnp.int32, a.shape, 1)
        shift = 1
        while shift < SUBLANES:
            keep = row >= shift
            a_prev = pltpu.roll(a, shift, axis=1)
            b_prev = pltpu.roll(b, shift, axis=1)
            b = jnp.where(keep, a * b_prev + b, b)
            a = jnp.where(keep, a * a_prev, a)
            shift *= 2
        a_scr[rows, :] = a.reshape(part, tc)
        b_scr[rows, :] = b.reshape(part, tc)

        for s, start, length in segs:
            lo = start if nseg == 1 else s * tseg
            if nseg > 1 or p == 0:
                h_run = h0_ref[s] if tiles_per_seq == 1 else hcar[c]

            for i in range(length // SUBLANES):
                off = lo + i * SUBLANES
                h = a_scr[off:off + SUBLANES, :] * h_run + b_scr[off:off + SUBLANES, :]
                b_scr[off:off + SUBLANES, :] = h
                h_run = h[SUBLANES - 1:SUBLANES, :]
            if nseg > 1 or p == n_parts - 1:
                hl_ref[s] = h_run
                if tiles_per_seq > 1:
                    hcar[c] = h_run

        out_ref[rows, :] += jnp.dot((b_scr[rows, :] * gelu_gate[p]).astype(BF16), wo_ref[...],
                                    preferred_element_type=F32)

    for s in range(nseg):
        new_hist = rbuf[s, pad + tseg:SUBLANES + tseg, :]
        nh_ref[s] = new_hist
        if tiles_per_seq > 1:
            ccar[c] = new_hist


def _rg_layer(x, hist, h0, t, norm_g, w_in, conv_w, conv_b, gate_w, gate_b, log_lam, w_out, cast=()):
    m, d = x.shape
    n_seq = m // t
    c_rnn = w_out.shape[0]
    conv_width = conv_w.shape[0]
    blk = c_rnn // RG_BLOCKS
    tc = min(RG_COL_TILE, c_rnn)
    n_c = c_rnn // tc
    assert n_c >= 2
    tseg, nseg, tps = _row_tiling(n_seq, t)
    tm = tseg * nseg
    kern = functools.partial(_rg_kernel, tseg=tseg, nseg=nseg, tiles_per_seq=tps, blk=blk,
                             part=_part_rows(tseg, nseg), n_cast=len(cast))
    seq_blk = lambda r, c: (r // tps, 0, c)
    cast_in, cast_out, cast_shape = _cast_specs(cast, (m // tm) * n_c, lambda r, c: r * n_c + c)
    out, new_hist, h_last, *narrowed = pl.pallas_call(
        kern,
        grid=(m // tm, n_c),
        in_specs=[
            pl.BlockSpec(memory_space=pl.ANY),
            pl.BlockSpec((nseg, conv_width - 1, tc), seq_blk),
            pl.BlockSpec((nseg, 1, tc), seq_blk),
            pl.BlockSpec((1, d), lambda r, c: (0, 0)),
            pl.BlockSpec((d, tc), lambda r, c: (0, c)),
            pl.BlockSpec((d, tc), lambda r, c: (0, n_c + c)),
            pl.BlockSpec((conv_width, tc), lambda r, c: (0, c)),
            pl.BlockSpec((1, tc), lambda r, c: (0, c)),
            pl.BlockSpec((tc // blk, blk, 2 * blk), lambda r, c: (c, 0, 0)),
            pl.BlockSpec((tc // blk, 1, 2 * blk), lambda r, c: (c, 0, 0)),
            pl.BlockSpec((1, tc), lambda r, c: (0, c)),
            pl.BlockSpec((tc, d), lambda r, c: (c, 0)),
        ] + cast_in,
        out_specs=[
            pl.BlockSpec((tm, d), lambda r, c: (r, 0)),
            pl.BlockSpec((nseg, conv_width - 1, tc), lambda r, c: (r, 0, c)),
            pl.BlockSpec((nseg, 1, tc), lambda r, c: (r, 0, c)),
        ] + cast_out,
        out_shape=[
            jax.ShapeDtypeStruct((m, d), F32),
            jax.ShapeDtypeStruct((n_seq * tps, conv_width - 1, c_rnn), F32),
            jax.ShapeDtypeStruct((n_seq * tps, 1, c_rnn), F32),
        ] + cast_shape,
        scratch_shapes=[
            pltpu.VMEM((tm, d), F32),
            pltpu.SemaphoreType.DMA,
            pltpu.VMEM((tm, d), BF16),
            pltpu.VMEM((nseg, SUBLANES + tseg, tc), F32),
            pltpu.VMEM((tm, tc), F32),
            pltpu.VMEM((tm, tc), F32),
            pltpu.VMEM((tm, tc), F32),
            pltpu.VMEM((n_c, conv_width - 1, tc), F32),
            pltpu.VMEM((n_c, 1, tc), F32),
        ],
        compiler_params=_params(2),
        name="rg_layer",
    )(x, hist, h0, norm_g.reshape(1, d), w_in, w_in, conv_w, conv_b.reshape(1, c_rnn),
      gate_w, gate_b.reshape(RG_BLOCKS, 1, 2 * blk), log_lam.reshape(1, c_rnn), w_out,
      *[w3 for w3, _ in cast])
    return out, _last_tile(new_hist, n_seq, tps), _last_tile(h_last, n_seq, tps), narrowed


def _ffn_kernel(x_hbm, hist_ref, g_ref, wg_ref, wv_ref, cwg_ref, cwv_ref, cbg_ref, cbv_ref, wd_ref,
                fg_ref, out_ref, nh_ref,
                x_buf, x_sem, xn_scr, ubuf, hid_scr, ccar,
                *, tseg, nseg, tiles_per_seq, final_norm, part):
    r = pl.program_id(0)
    c = pl.program_id(1)
    n_c = pl.num_programs(1)
    conv_w = cwg_ref.shape[0]
    tm = tseg * nseg
    tb = r % tiles_per_seq
    pad = SUBLANES - (conv_w - 1)
    halves = ((wg_ref, cwg_ref, cbg_ref), (wv_ref, cwv_ref, cbv_ref))

    def consume():
        x = x_buf[...]
        xn_scr[...] = _rmsnorm_rows(x, g_ref[...]).astype(BF16)
        out_ref[...] = x

    _stream_x_tile(x_hbm, x_buf, x_sem, consume)

    for s in range(nseg):
        for half in range(2):
            def load_state(half=half, s=s):
                for k in range(conv_w - 1):
                    ubuf[half, s, pad + k:pad + k + 1, :] = hist_ref[s, k, half:half + 1, :]

            if tiles_per_seq == 1:
                load_state()
            else:
                pl.when(tb == 0)(load_state)

                @pl.when(tb != 0)
                def _():
                    ubuf[half, s, pad:SUBLANES, :] = ccar[c, half]

    n_parts = tm // part
    for p in range(n_parts):
        xn = xn_scr[p * part:(p + 1) * part, :]
        for half, (w_ref, _, _) in enumerate(halves):
            u = jnp.dot(xn, w_ref[...], preferred_element_type=F32)
            if nseg == 1:
                ubuf[half, 0, SUBLANES + p * part:SUBLANES + (p + 1) * part, :] = u
            else:
                for s in range(nseg):
                    ubuf[half, s, SUBLANES:SUBLANES + tseg, :] = u[s * tseg:(s + 1) * tseg]

    for p in range(n_parts):
        rows = slice(p * part, (p + 1) * part)
        segs = [(0, p * part, part)] if nseg == 1 else [(s, 0, tseg) for s in range(nseg)]
        for s, start, length in segs:
            conv = []
            for half, (_, cw_ref, cb_ref) in enumerate(halves):
                cv = cb_ref[...] + ubuf[half, s, pad + start:pad + start + length, :] * cw_ref[0:1, :]
                for k in range(1, conv_w):
                    lo = pad + start + k
                    cv = cv + ubuf[half, s, lo:lo + length, :] * cw_ref[k:k + 1, :]
                conv.append(cv)
            hid = (jax.nn.gelu(conv[0]) * conv[1]).astype(BF16)
            if nseg > 1:
                hid_scr[s * tseg:(s + 1) * tseg, :] = hid
        if nseg > 1:
            hid = hid_scr[...]
        out_ref[rows, :] += jnp.dot(hid, wd_ref[...], preferred_element_type=F32)

    for s in range(nseg):
        for half in range(2):
            for k in range(conv_w - 1):
                nh_ref[s, k, half:half + 1, :] = ubuf[half, s, pad + tseg + k:pad + tseg + k + 1, :]
            if tiles_per_seq > 1:
                ccar[c, half] = ubuf[half, s, pad + tseg:SUBLANES + tseg, :]

    if final_norm:
        @pl.when(c == n_c - 1)
        def _():
            out_ref[...] = _rmsnorm_rows(out_ref[...], fg_ref[...])


def _ffn_layer(x, hist, t, norm_g, w_up, conv_w, conv_b, w_down, final_g):
    m, d = x.shape
    n_seq = m // t
    d_ff = w_down.shape[0]
    conv_width = conv_w.shape[0]
    tf = FF_COL_TILE
    assert d_ff % tf == 0
    n_c = d_ff // tf
    assert n_c >= 2
    tseg, nseg, tps = _row_tiling(n_seq, t)
    tm = tseg * nseg
    final_norm = final_g is not None
    if final_g is None:
        final_g = jnp.ones((d,), F32)
    kern = functools.partial(_ffn_kernel, tseg=tseg, nseg=nseg, tiles_per_seq=tps,
                             final_norm=final_norm, part=_part_rows(tseg, nseg))
    hist4 = hist.reshape(n_seq, conv_width - 1, 2, d_ff)
    conv_b2 = conv_b.reshape(1, 2 * d_ff)
    out, new_hist = pl.pallas_call(
        kern,
        grid=(m // tm, n_c),
        in_specs=[
            pl.BlockSpec(memory_space=pl.ANY),
            pl.BlockSpec((nseg, conv_width - 1, 2, tf), lambda r, c: (r // tps, 0, 0, c)),
            pl.BlockSpec((1, d), lambda r, c: (0, 0)),
            pl.BlockSpec((d, tf), lambda r, c: (0, c)),
            pl.BlockSpec((d, tf), lambda r, c: (0, n_c + c)),
            pl.BlockSpec((conv_width, tf), lambda r, c: (0, c)),
            pl.BlockSpec((conv_width, tf), lambda r, c: (0, n_c + c)),
            pl.BlockSpec((1, tf), lambda r, c: (0, c)),
            pl.BlockSpec((1, tf), lambda r, c: (0, n_c + c)),
            pl.BlockSpec((tf, d), lambda r, c: (c, 0)),
            pl.BlockSpec((1, d), lambda r, c: (0, 0)),
        ],
        out_specs=[
            pl.BlockSpec((tm, d), lambda r, c: (r, 0)),
            pl.BlockSpec((nseg, conv_width - 1, 2, tf), lambda r, c: (r, 0, 0, c)),
        ],
        out_shape=[
            jax.ShapeDtypeStruct((m, d), F32),
            jax.ShapeDtypeStruct((n_seq * tps, conv_width - 1, 2, d_ff), F32),
        ],
        scratch_shapes=[
            pltpu.VMEM((tm, d), F32),
            pltpu.SemaphoreType.DMA,
            pltpu.VMEM((tm, d), BF16),
            pltpu.VMEM((2, nseg, SUBLANES + tseg, tf), F32),
            pltpu.VMEM((tm if nseg > 1 else SUBLANES, tf), BF16),
            pltpu.VMEM((n_c, 2, conv_width - 1, tf), F32),
        ],
        compiler_params=_params(2),
        name="conv_ffn",
    )(x, hist4, norm_g.reshape(1, d), w_up, w_up, conv_w, conv_w, conv_b2, conv_b2, w_down,
      final_g.reshape(1, d))
    return out, _last_tile(new_hist, n_seq, tps).reshape(n_seq, conv_width - 1, 2 * d_ff)


def _q_kernel(x_ref, g_ref, w_ref, q_ref, xn_ref, *, scale):
    xn = _rmsnorm_rows(x_ref[...], g_ref[...]).astype(BF16)
    xn_ref[...] = xn
    q_ref[...] = (jnp.dot(xn, w_ref[...], preferred_element_type=F32) * scale).astype(BF16)


def _q_proj(x, norm_g, w_qkv, scale):
    m, d = x.shape
    d_att = w_qkv.shape[1] // 3
    tm = min(QKV_ROW_TILE, m)
    return pl.pallas_call(
        functools.partial(_q_kernel, scale=scale),
        grid=(m // tm,),
        in_specs=[
            pl.BlockSpec((tm, d), lambda r: (r, 0)),
            pl.BlockSpec((1, d), lambda r: (0, 0)),
            pl.BlockSpec((d, d_att), lambda r: (0, 0), pipeline_mode=pl.Buffered(1)),
        ],
        out_specs=[
            pl.BlockSpec((tm, d_att), lambda r: (r, 0)),
            pl.BlockSpec((tm, d), lambda r: (r, 0)),
        ],
        out_shape=[
            jax.ShapeDtypeStruct((m, d_att), BF16),
            jax.ShapeDtypeStruct((m, d), BF16),
        ],
        compiler_params=_params(1),
        name="q_proj",
    )(x, norm_g.reshape(1, d), w_qkv)


def _kv_kernel(xn_ref, wk_ref, wv_ref, k_heads_ref, k_flat_ref, v_heads_ref, v_flat_ref):
    xn = xn_ref[...]
    for w_ref, heads_ref, flat_ref in ((wk_ref, k_heads_ref, k_flat_ref),
                                       (wv_ref, v_heads_ref, v_flat_ref)):
        y = jnp.dot(xn, w_ref[...], preferred_element_type=F32)
        flat_ref[...] = y.astype(BF16)
        heads_ref[...] = pltpu.einshape("m(hd)->mhd", y, h=heads_ref.shape[1])


def _kv_proj(xn, w_qkv, k_split, v_split):
    m, d = xn.shape
    d_att = w_qkv.shape[1] // 3
    tm = min(KV_ROW_TILE, m)
    flat = pl.BlockSpec((tm, d_att), lambda r: (r, 0))
    heads = lambda n: pl.BlockSpec((tm, n, d_att // n), lambda r: (r, 0, 0))
    return pl.pallas_call(
        _kv_kernel,
        grid=(m // tm,),
        in_specs=[
            pl.BlockSpec((tm, d), lambda r: (r, 0)),
            pl.BlockSpec((d, d_att), lambda r: (0, 1), pipeline_mode=pl.Buffered(1)),
            pl.BlockSpec((d, d_att), lambda r: (0, 2), pipeline_mode=pl.Buffered(1)),
        ],
        out_specs=[heads(k_split), flat, heads(v_split), flat],
        out_shape=[
            jax.ShapeDtypeStruct((m, k_split, d_att // k_split), F32),
            jax.ShapeDtypeStruct((m, d_att), BF16),
            jax.ShapeDtypeStruct((m, v_split, d_att // v_split), F32),
            jax.ShapeDtypeStruct((m, d_att), BF16),
        ],
        compiler_params=_params(1),
        name="kv_proj",
    )(xn, w_qkv, w_qkv)


def _diff_lambda(lam_ref, lambda_init):
    lam = lam_ref[...]
    s1 = jnp.sum(lam[0:1] * lam[1:2], axis=-1, keepdims=True)
    s2 = jnp.sum(lam[2:3] * lam[3:4], axis=-1, keepdims=True)
    return jnp.exp(s1) - jnp.exp(s2) + lambda_init


def _head_norm(o, subln_ref, lambda_init):
    return _rmsnorm_rows(o, subln_ref[...]) * (1.0 - lambda_init)


def _nt_dot(a, b):
    return lax.dot_general(a, b, (((1,), (1,)), ((), ())), preferred_element_type=F32)


def _attn_prompt_kernel(q_ref, k_ref, v_ref, lam_ref, subln_ref, *refs, lambda_init, tq, n_cast):
    o_ref = refs[n_cast]
    _narrow_blocks(refs[:n_cast], refs[n_cast + 1:])
    t, hw = q_ref.shape
    hd = hw // 2
    lam = _diff_lambda(lam_ref, lambda_init)

    def fold(x, op):
        acc = x[:, 0:LANES]
        for g in range(1, x.shape[1] // LANES):
            acc = op(acc, x[:, g * LANES:(g + 1) * LANES])
        return acc

    def spread(x, width):
        return jnp.concatenate([x] * (width // LANES), axis=1)

    def row_max(s):
        return jnp.broadcast_to(jnp.max(fold(s, jnp.maximum), axis=1, keepdims=True),
                                (s.shape[0], LANES))

    q_chunk = lax.broadcasted_iota(jnp.int32, (tq, tq), 0) // CHUNK
    k_chunk = lax.broadcasted_iota(jnp.int32, (tq, tq), 1) // CHUNK
    visible = k_chunk <= q_chunk
    for i in range(t // tq):
        rows = slice(i * tq, (i + 1) * tq)
        outs = []
        for j in range(2):
            cols = slice(j * hd, (j + 1) * hd)
            q = q_ref[rows, cols]
            s = jnp.where(visible, _nt_dot(q, k_ref[rows, cols]), NEG_INF)
            m = row_max(s)
            p = jnp.exp2(s - spread(m, tq))
            l = fold(p, jnp.add)
            acc = jnp.dot(p.astype(BF16), v_ref[rows, :], preferred_element_type=F32)
            for kb in range(i):
                keys = slice(kb * tq, (kb + 1) * tq)
                s = _nt_dot(q, k_ref[keys, cols])
                m_new = jnp.maximum(m, row_max(s))
                alpha = jnp.exp2(m - m_new)
                p = jnp.exp2(s - spread(m_new, tq))
                l = alpha * l + fold(p, jnp.add)
                acc = spread(alpha, hw) * acc + jnp.dot(p.astype(BF16), v_ref[keys, :],
                                                        preferred_element_type=F32)
                m = m_new
            outs.append(acc / jnp.sum(l, axis=1, keepdims=True))
        o = outs[0] - lam * outs[1]
        o_ref[rows, :] = _head_norm(o, subln_ref, lambda_init).astype(BF16)


def _attn_prompt(q, kb, vb, t, lam_p, subln, lambda_init, cast=()):
    m, d_att = q.shape
    n_seq = m // t
    hw = d_att // N_HEADS
    tq = min(ATTN_TILE, t)
    assert t % tq == 0 and tq % CHUNK == 0
    seq_head = pl.BlockSpec((t, hw), lambda b, h: (b, h))
    cast_in, cast_out, cast_shape = _cast_specs(cast, n_seq * N_HEADS, lambda b, h: b * N_HEADS + h)
    o, *narrowed = pl.pallas_call(
        functools.partial(_attn_prompt_kernel, lambda_init=lambda_init, tq=tq, n_cast=len(cast)),
        grid=(n_seq, N_HEADS),
        in_specs=[
            seq_head, seq_head, seq_head,
            pl.BlockSpec(lam_p.shape, lambda b, h: (0, 0)),
            pl.BlockSpec((1, hw), lambda b, h: (0, 0)),
        ] + cast_in,
        out_specs=[seq_head] + cast_out,
        out_shape=[jax.ShapeDtypeStruct((m, d_att), BF16)] + cast_shape,
        compiler_params=_params(2),
        name="attn_prompt",
    )(q, kb, vb, lam_p, subln.reshape(1, hw), *[w3 for w3, _ in cast])
    return o, narrowed


def _attn_sample_kernel(q_ref, ck_ref, cv_ref, kn_ref, vn_ref, lam_ref, subln_ref, o_ref,
                        m_scr, l_scr, acc_scr, *, lambda_init):
    c = pl.program_id(1)
    n_c = pl.num_programs(1)
    t, d_att = q_ref.shape
    hw = d_att // N_HEADS
    hd = hw // 2
    q = q_ref[...]

    @pl.when(c == 0)
    def _():
        m_scr[...] = jnp.full(m_scr.shape, NEG_INF, F32)
        l_scr[...] = jnp.zeros_like(l_scr)
        acc_scr[...] = jnp.zeros_like(acc_scr)

    def attend(k_all, v_all):
        n = k_all.shape[0]
        for h in range(N_HEADS):
            v_h = v_all[:, h * hw:(h + 1) * hw]
            for j in range(2):
                hm = 2 * h + j
                cols = slice(h * hw + j * hd, h * hw + (j + 1) * hd)
                s = _nt_dot(q[:, cols], k_all[:, cols])
                m_old = m_scr[hm]
                m_new = jnp.maximum(m_old, jnp.max(s, axis=1, keepdims=True))
                alpha = jnp.exp2(m_old - m_new)
                p = jnp.exp2(s - m_new[:, 0:1])
                if n % LANES == 0:
                    psum = p[:, 0:LANES]
                    for g in range(1, n // LANES):
                        psum = psum + p[:, g * LANES:(g + 1) * LANES]
                else:
                    psum = jnp.sum(p, axis=1, keepdims=True) * (1.0 / LANES)
                m_scr[hm] = m_new
                l_scr[hm] = alpha * l_scr[hm] + psum
                acc_scr[hm] = (jnp.concatenate([alpha] * (hw // LANES), axis=1) * acc_scr[hm]
                               + jnp.dot(p.astype(BF16), v_h, preferred_element_type=F32))

    attend(pltpu.einshape("mhd->m(hd)", ck_ref[0, 0]).astype(BF16),
           pltpu.einshape("mhd->m(hd)", cv_ref[0, 0]).astype(BF16))

    @pl.when(c == n_c - 1)
    def _():
        attend(kn_ref[...], vn_ref[...])
        lam = _diff_lambda(lam_ref, lambda_init)
        for h in range(N_HEADS):
            outs = [acc_scr[2 * h + j] / jnp.sum(l_scr[2 * h + j], axis=1, keepdims=True)
                    for j in range(2)]
            o = outs[0] - lam * outs[1]
            o_ref[:, h * hw:(h + 1) * hw] = _head_norm(o, subln_ref, lambda_init).astype(BF16)


def _attn_sample(q, cache_k, cache_v, layer, kb, vb, t, lam_p, subln, lambda_init):
    m, d_att = q.shape
    n_seq = m // t
    past = cache_k.shape[2]
    hw = d_att // N_HEADS
    pc = min(SAMPLE_KEY_TILE, past)
    assert past % pc == 0
    new_blk = pl.BlockSpec((t, d_att), lambda b, c: (b, 0))
    cache_blk = lambda arr: pl.BlockSpec((1, 1, pc) + arr.shape[3:], lambda b, c: (layer, b, c, 0, 0))
    return pl.pallas_call(
        functools.partial(_attn_sample_kernel, lambda_init=lambda_init),
        grid=(n_seq, past // pc),
        in_specs=[
            new_blk, cache_blk(cache_k), cache_blk(cache_v), new_blk, new_blk,
            pl.BlockSpec(lam_p.shape, lambda b, c: (0, 0)),
            pl.BlockSpec((1, hw), lambda b, c: (0, 0)),
        ],
        out_specs=new_blk,
        out_shape=jax.ShapeDtypeStruct((m, d_att), BF16),
        scratch_shapes=[
            pltpu.VMEM((2 * N_HEADS, t, LANES), F32),
            pltpu.VMEM((2 * N_HEADS, t, LANES), F32),
            pltpu.VMEM((2 * N_HEADS, t, hw), F32),
        ],
        compiler_params=_params(2),
        name="attn_sample",
    )(q, cache_k, cache_v, kb, vb, lam_p, subln.reshape(1, hw))


def _proj_residual_kernel(o_ref, w_ref, x_ref, out_ref):
    out_ref[...] = x_ref[...] + jnp.dot(o_ref[...], w_ref[...], preferred_element_type=F32)


def _proj_residual(o, w, x):
    m, kdim = o.shape
    n = w.shape[1]
    tm = min(QKV_ROW_TILE, m)
    return pl.pallas_call(
        _proj_residual_kernel,
        grid=(m // tm,),
        in_specs=[
            pl.BlockSpec((tm, kdim), lambda r: (r, 0)),
            pl.BlockSpec((kdim, n), lambda r: (0, 0), pipeline_mode=pl.Buffered(1)),
            pl.BlockSpec((tm, n), lambda r: (r, 0)),
        ],
        out_specs=pl.BlockSpec((tm, n), lambda r: (r, 0)),
        out_shape=jax.ShapeDtypeStruct((m, n), F32),
        compiler_params=_params(1),
        name="attn_out_proj",
    )(o, w, x)


def kernel(x_prompt, x_sample, state_rglru_conv, state_rglru_h, cache_attn_k, cache_attn_v, state_ffn_conv, rg_norm, rg_w_in, rg_conv_w, rg_conv_b, rg_gate_w, rg_gate_b, rg_log_lambda, rg_w_out, at_norm, at_w_qkv, at_lambda, at_subln, at_w_out, ffn_norm, ffn_w_up, ffn_conv_w, ffn_conv_b, ffn_w_down, final_norm):
    bp, tp, d = x_prompt.shape
    bs, ts, _ = x_sample.shape
    depth = ffn_norm.shape[0]
    c_rnn = rg_w_out.shape[1]
    d_att = at_w_out.shape[1]
    head_dim = d_att // (2 * N_HEADS)

    xp = x_prompt.reshape(bp * tp, d)
    xs = x_sample.reshape(bs * ts, d)
    p_rg_conv, p_rg_h, p_k, p_v, p_ffn = [], [], [], [], []
    s_rg_conv, s_rg_h, s_k, s_v, s_ffn = [], [], [], [], []
    ffn_narrowed = {}
    for layer in range(depth):
        j = layer // 2
        if layer % 2 == 0:
            w = (rg_norm[j], rg_w_in[j].astype(BF16), rg_conv_w[j], rg_conv_b[j],
                 rg_gate_w[j].astype(BF16), rg_gate_b[j], rg_log_lambda[j], rg_w_out[j].astype(BF16))
            zero_hist = jnp.zeros((bp, rg_conv_w.shape[1] - 1, c_rnn), F32)
            zero_h = jnp.zeros((bp, 1, c_rnn), F32)
            xp, cp, hp, ffn_narrowed[layer] = _rg_layer(
                xp, zero_hist, zero_h, tp, *w, cast=((ffn_w_up, layer), (ffn_w_down, layer)))
            xs, cs, hs, _ = _rg_layer(xs, state_rglru_conv[j],
                                      state_rglru_h[j].reshape(bs, 1, c_rnn), ts, *w)
            p_rg_conv.append(cp); p_rg_h.append(hp.reshape(bp, c_rnn))
            s_rg_conv.append(cs); s_rg_h.append(hs.reshape(bs, c_rnn))
        else:
            lambda_init = 0.8 - 0.6 * math.exp(-0.3 * layer)
            w_qkv = at_w_qkv[j].astype(BF16)
            w_out = at_w_out[j].astype(BF16)
            scale = head_dim ** -0.5 * math.log2(math.e)
            qp, xnp = _q_proj(xp, at_norm[j], w_qkv, scale)
            kp, kpb, vp, vpb = _kv_proj(xnp, w_qkv, 2 * N_HEADS, N_HEADS)
            op, ffn_narrowed[layer] = _attn_prompt(
                qp, kpb, vpb, tp, at_lambda[j], at_subln[j], lambda_init,
                cast=((ffn_w_up, layer), (ffn_w_down, layer)))
            xp = _proj_residual(op, w_out, xp)
            qs, xns = _q_proj(xs, at_norm[j], w_qkv, scale)
            kn, knb, vn, vnb = _kv_proj(xns, w_qkv, 2 * N_HEADS, N_HEADS)
            os_ = _attn_sample(qs, cache_attn_k, cache_attn_v, j, knb, vnb, ts,
                               at_lambda[j], at_subln[j], lambda_init)
            xs = _proj_residual(os_, w_out, xs)
            p_k.append(kp.reshape(bp, tp, 2 * N_HEADS, head_dim))
            p_v.append(vp.reshape(bp, tp, N_HEADS, 2 * head_dim))
            s_k.append(kn.reshape(bs, ts, 2 * N_HEADS, head_dim))
            s_v.append(vn.reshape(bs, ts, N_HEADS, 2 * head_dim))
        last = layer == depth - 1
        w_up, w_down = ffn_narrowed.get(layer) or (ffn_w_up[layer].astype(BF16),
                                                   ffn_w_down[layer].astype(BF16))
        fw = (ffn_norm[layer], w_up, ffn_conv_w[layer], ffn_conv_b[layer], w_down,
              final_norm if last else None)
        zero_ffn = jnp.zeros((bp,) + state_ffn_conv.shape[2:], F32)
        xp, fcp = _ffn_layer(xp, zero_ffn, tp, *fw)
        xs, fcs = _ffn_layer(xs, state_ffn_conv[layer], ts, *fw)
        p_ffn.append(fcp); s_ffn.append(fcs)
    stack = lambda arrs: arrs[0][None] if len(arrs) == 1 else jnp.stack(arrs)
    return (xp.reshape(bp, tp, d), xs.reshape(bs, ts, d),
            stack(p_rg_conv), stack(p_rg_h), stack(p_k), stack(p_v), stack(p_ffn),
            stack(s_rg_conv), stack(s_rg_h), stack(s_k), stack(s_v), stack(s_ffn))
```
